```python
import math
import jax, jax.numpy as jnp
from jax import lax
import numpy as np

D_MODEL = 1024
BATCH = 2
SEQ = 8192
DEPTH = 2
DEC_BATCH = 32
DEC_SEQ = 8
PAST_LEN = 16384
PAGE_SIZE = 128

H_A = 4
DK_A = 128
DV_A = 128
CONV_W = 4
CHUNK_A = 64
H_B = 4
DK_B = 128
DV_B = 64
CHUNK_B = 16
H_C = 4
HD_C = 64
DIL_GROUPS = ((128, 1), (512, 4), (2048, 16))
MAX_WINDOW = 2048
Q_BLOCK = 128
N_KEYS = 128
N_EXPERTS = N_KEYS * N_KEYS
N_PH = 8
D_QUERY = 256
TOPK = 16
PEER_BLOCK = 128

CONV_CH = 2 * H_A * DK_A + H_A * DV_A
MIX_WIDTH = H_A * DV_A + H_B * DV_B + H_C * HD_C
IN_SIZES = (CONV_CH, H_A * DV_A, H_A, H_A, H_B * DK_B, H_B * DK_B, H_B * DV_B, H_B * DV_B, 3 * H_C * HD_C)
IN_WIDTH = sum(IN_SIZES)
SPLITS = [int(s) for s in np.cumsum(IN_SIZES)[:-1]]

kernel_name = "hymba_delta_hgrn2_dilswa_peer_step"

F32 = jnp.float32


def rmsnorm(x, w, eps=1e-6):
    xf = x.astype(F32)
    y = xf * lax.rsqrt(jnp.mean(xf * xf, axis=-1, keepdims=True) + eps) * w.astype(F32)
    return y.astype(x.dtype)


def l2norm(x, eps=1e-6):
    return x * lax.rsqrt(jnp.sum(x * x, axis=-1, keepdims=True) + eps)


def pad_seq(t, pad):
    return jnp.pad(t, [(0, 0), (0, pad)] + [(0, 0)] * (t.ndim - 2))


def to_chunks(t, c):
    b, lp = t.shape[:2]
    t = t.reshape((b, lp // c, c) + t.shape[2:])
    return jnp.moveaxis(t, (1, 3), (0, 2))


def from_chunks(t):
    t = jnp.moveaxis(t, (0, 2), (1, 3))
    b, n, c = t.shape[:3]
    return t.reshape((b, n * c) + t.shape[3:])


def causal_conv(x, buf, w):
    L = x.shape[1]
    xp = jnp.concatenate([buf.astype(x.dtype), x], axis=1)
    y = xp[:, 0:L] * w[0]
    for j in range(1, CONV_W):
        y = y + xp[:, j:j + L] * w[j]
    return y, xp[:, -(CONV_W - 1):]


def gated_delta_rule(q, k, v, beta, g, S0):
    B, L, H, dk = q.shape
    c = min(CHUNK_A, L)
    pad = (-L) % c
    q, k, v, beta, g = [pad_seq(t, pad) for t in (q * dk ** -0.5, k, v, beta, g)]
    qc, kc, vc = to_chunks(q, c), to_chunks(k, c), to_chunks(v, c)
    bc = to_chunks(beta, c)
    gcum = jnp.cumsum(to_chunks(g, c), axis=-1)
    strict = jnp.tril(jnp.ones((c, c), bool), -1)
    incl = jnp.tril(jnp.ones((c, c), bool))
    diff = gcum[..., :, None] - gcum[..., None, :]
    d_strict = jnp.exp(jnp.where(strict, diff, -jnp.inf))
    d_incl = jnp.exp(jnp.where(incl, diff, -jnp.inf))
    kb = kc * bc[..., None]
    a_mat = jnp.eye(c, dtype=F32) + jnp.einsum('nbhtk,nbhsk->nbhts', kb, kc) * d_strict
    u_c = lax.linalg.triangular_solve(a_mat, vc * bc[..., None], left_side=True, lower=True, unit_diagonal=True)
    w_c = lax.linalg.triangular_solve(a_mat, kb * jnp.exp(gcum)[..., None], left_side=True, lower=True, unit_diagonal=True)
    att = jnp.einsum('nbhtk,nbhsk->nbhts', qc, kc) * d_incl
    qg = qc * jnp.exp(gcum)[..., None]
    kg = kc * jnp.exp(gcum[..., -1:] - gcum)[..., None]
    glast = jnp.exp(gcum[..., -1])

    def step(S, xs):
        u_, w_, att_, qg_, kg_, gl_ = xs
        v_new = u_ - jnp.einsum('bhtk,bhkv->bhtv', w_, S)
        o = jnp.einsum('bhtk,bhkv->bhtv', qg_, S) + jnp.einsum('bhts,bhsv->bhtv', att_, v_new)
        S = S * gl_[..., None, None] + jnp.einsum('bhsk,bhsv->bhkv', kg_, v_new)
        return S, o

    S, o = lax.scan(step, S0, (u_c, w_c, att, qg, kg, glast))
    return from_chunks(o)[:, :L], S


def hgrn2_recurrence(q, k, v, logf, S0):
    B, L, H, dk = q.shape
    c = min(CHUNK_B, L)
    pad = (-L) % c
    q, k, v, logf = [pad_seq(t, pad) for t in (q * dk ** -0.5, k, v, logf)]
    qc, kc, vc = to_chunks(q, c), to_chunks(k, c), to_chunks(v, c)
    gc = jnp.cumsum(to_chunks(logf, c), axis=3)
    tri = jnp.tril(jnp.ones((c, c), bool))[:, :, None]

    def step(S, xs):
        q_, k_, v_, g_ = xs
        decay = jnp.exp(jnp.where(tri, g_[:, :, :, None, :] - g_[:, :, None, :, :], -jnp.inf))
        att = jnp.einsum('bhtk,bhsk,bhtsk->bhts', q_, k_, decay)
        o = jnp.einsum('bhtk,bhkv->bhtv', q_ * jnp.exp(g_), S) + jnp.einsum('bhts,bhsv->bhtv', att, v_)
        g_last = g_[:, :, -1:, :]
        S = S * jnp.exp(g_last)[:, :, 0, :, None] + jnp.einsum('bhsk,bhsv->bhkv', k_ * jnp.exp(g_last - g_), v_)
        return S, o

    S, o = lax.scan(step, S0, (qc, kc, vc, gc))
    return from_chunks(o)[:, :L], S


def dilated_window_attention(q, k, v, q_start):
    B, Lq, H, hd = q.shape
    Lk = k.shape[1]
    blk = min(Q_BLOCK, Lq)
    pad = (-Lq) % blk
    nb = (Lq + pad) // blk
    qb = jnp.moveaxis(pad_seq(q, pad).reshape(B, nb, blk, H, hd), 1, 0)
    starts = q_start + jnp.arange(nb, dtype=jnp.int32) * blk
    scale = hd ** -0.5

    def block(args):
        qblk, s0 = args
        qpos = s0 + jnp.arange(blk, dtype=jnp.int32)
        outs, lses = [], []
        for window, dil in DIL_GROUPS:
            idx = qpos[:, None] - dil * jnp.arange(window // dil + 1, dtype=jnp.int32)[None, :]
            valid = idx >= 0
            idx = jnp.clip(idx, 0, Lk - 1)
            kg = k[:, idx]
            vg = v[:, idx]
            s = jnp.einsum('bqhd,bqmhd->bqhm', qblk, kg).astype(F32) * scale
            s = jnp.where(valid[None, :, None, :], s, -jnp.inf)
            mx = jnp.max(s, axis=-1, keepdims=True)
            e = jnp.exp(s - mx)
            den = jnp.sum(e, axis=-1, keepdims=True)
            outs.append(jnp.einsum('bqhm,bqmhd->bqhd', e / den, vg.astype(F32)))
            lses.append((mx + jnp.log(den))[..., 0])
        wts = jax.nn.softmax(jnp.stack(lses, axis=-1), axis=-1)
        return jnp.einsum('bqhg,gbqhd->bqhd', wts, jnp.stack(outs, axis=0))

    out = lax.map(block, (qb, starts))
    return jnp.moveaxis(out, 0, 1).reshape(B, nb * blk, H, hd)[:, :Lq]


def peer_ffn(h, wq, keys, u, v):
    B, L, D = h.shape
    T = B * L
    hf = h.reshape(T, D)
    q = (hf @ wq).reshape(T, N_PH, 2, D_QUERY // 2)
    s = jnp.einsum('thpk,hpnk->thpn', q, keys).astype(F32)
    s_top, i_top = lax.top_k(s, TOPK)
    cand = (s_top[:, :, 0, :, None] + s_top[:, :, 1, None, :]).reshape(T, N_PH, TOPK * TOPK)
    cidx = (i_top[:, :, 0, :, None] * N_KEYS + i_top[:, :, 1, None, :]).reshape(T, N_PH, TOPK * TOPK)
    best, pos = lax.top_k(cand, TOPK)
    eidx = jnp.take_along_axis(cidx, pos, axis=-1).reshape(T, N_PH * TOPK)
    gate = jax.nn.softmax(best, axis=-1).reshape(T, N_PH * TOPK)
    pad = (-T) % PEER_BLOCK
    nb = (T + pad) // PEER_BLOCK
    hp = jnp.pad(hf, ((0, pad), (0, 0))).reshape(nb, PEER_BLOCK, D)
    ip = jnp.pad(eidx, ((0, pad), (0, 0))).reshape(nb, PEER_BLOCK, N_PH * TOPK)
    gp = jnp.pad(gate, ((0, pad), (0, 0))).reshape(nb, PEER_BLOCK, N_PH * TOPK)

    def block(args):
        hb, ib, gb = args
        act = jax.nn.gelu(jnp.einsum('td,tkd->tk', hb, u[ib]).astype(F32), approximate=False)
        return jnp.einsum('tk,tkd->td', (gb * act).astype(hb.dtype), v[ib])

    out = lax.map(block, (hp, ip, gp))
    return out.reshape(nb * PEER_BLOCK, D)[:T].reshape(B, L, D).astype(h.dtype)


def trunk_layer(x, conv_buf, s_delta, s_hgrn, k_buf, v_buf,
                norm1_w, w_in, conv_a_w, a_log, dt_bias, onorm_a_w, lb, onorm_b_w, onorm_c_w,
                w_out, norm2_w, peer_wq, peer_keys, peer_u, peer_v):
    B, L, _ = x.shape
    h = rmsnorm(x, norm1_w)
    proj = h @ w_in
    a_qkv, a_z, a_b, a_a, b_q, b_f, b_i, b_g, c_qkv = jnp.split(proj, SPLITS, axis=-1)

    a_qkv, new_conv = causal_conv(a_qkv, conv_buf, conv_a_w)
    a_qkv = jax.nn.silu(a_qkv.astype(F32))
    aq, ak, av = jnp.split(a_qkv, [H_A * DK_A, 2 * H_A * DK_A], axis=-1)
    aq = l2norm(aq.reshape(B, L, H_A, DK_A))
    ak = l2norm(ak.reshape(B, L, H_A, DK_A))
    av = av.reshape(B, L, H_A, DV_A)
    beta = jax.nn.sigmoid(a_b.astype(F32))
    g = -jnp.exp(a_log.astype(F32)) * jax.nn.softplus(a_a.astype(F32) + dt_bias.astype(F32))
    o_a, new_delta = gated_delta_rule(aq, ak, av, beta, g, s_delta.astype(F32))
    o_a = rmsnorm(o_a, onorm_a_w) * jax.nn.silu(a_z.astype(F32).reshape(B, L, H_A, DV_A))

    z = b_f.astype(F32).reshape(B, L, H_B, DK_B)
    lb = lb.reshape(H_B, DK_B)
    logf = jnp.logaddexp(jnp.log(lb), jnp.log1p(-lb) + jax.nn.log_sigmoid(z))
    bk = (1.0 - lb) * jax.nn.sigmoid(-z)
    o_b, new_hgrn = hgrn2_recurrence(b_q.astype(F32).reshape(B, L, H_B, DK_B), bk,
                                     b_i.astype(F32).reshape(B, L, H_B, DV_B), logf, s_hgrn.astype(F32))
    o_b = rmsnorm(o_b, onorm_b_w) * jax.nn.silu(b_g.astype(F32).reshape(B, L, H_B, DV_B))

    cqkv = c_qkv.reshape(B, L, 3, H_C, HD_C)
    cq, ck, cv = cqkv[:, :, 0], cqkv[:, :, 1], cqkv[:, :, 2]
    k_all = jnp.concatenate([k_buf.astype(ck.dtype), ck], axis=1)
    v_all = jnp.concatenate([v_buf.astype(cv.dtype), cv], axis=1)
    o_c = rmsnorm(dilated_window_attention(cq, k_all, v_all, k_buf.shape[1]), onorm_c_w)
    keep = min(MAX_WINDOW, k_all.shape[1])
    new_k = k_all[:, -keep:]
    new_v = v_all[:, -keep:]

    mix = jnp.concatenate([o_a.reshape(B, L, -1), o_b.reshape(B, L, -1),
                           o_c.reshape(B, L, -1).astype(F32)], axis=-1).astype(x.dtype)
    x = x + mix @ w_out
    x = x + peer_ffn(rmsnorm(x, norm2_w), peer_wq, peer_keys, peer_u, peer_v)
    return x, new_conv, new_delta, new_hgrn, new_k, new_v


def setup_inputs(seed: int = 0) -> dict:
    key = jax.random.key(seed)
    ks = jax.random.split(key, 24)
    swa_buf = min(MAX_WINDOW, PAST_LEN)

    def nrm(k, shape, scale):
        return jax.random.normal(k, shape, F32) * scale

    dt = jnp.exp(jax.random.uniform(ks[10], (DEPTH, H_A), F32, math.log(1e-3), math.log(1e-1)))
    return {
        'x_prompt': nrm(ks[0], (BATCH, SEQ, D_MODEL), 1.0),
        'x_sample': nrm(ks[1], (DEC_BATCH, DEC_SEQ, D_MODEL), 1.0),
        'state_conv_a': nrm(ks[2], (DEPTH, DEC_BATCH, CONV_W - 1, CONV_CH), 1.0),
        'state_delta': nrm(ks[3], (DEPTH, DEC_BATCH, H_A, DK_A, DV_A), 0.05),
        'state_hgrn': nrm(ks[4], (DEPTH, DEC_BATCH, H_B, DK_B, DV_B), 0.5),
        'cache_swa_k': nrm(ks[5], (DEPTH, DEC_BATCH, swa_buf, H_C, HD_C), 1.0),
        'cache_swa_v': nrm(ks[6], (DEPTH, DEC_BATCH, swa_buf, H_C, HD_C), 1.0),
        'norm1_w': 1.0 + nrm(ks[7], (DEPTH, D_MODEL), 0.02),
        'w_in': nrm(ks[8], (DEPTH, D_MODEL, IN_WIDTH), D_MODEL ** -0.5),
        'conv_a_w': nrm(ks[9], (DEPTH, CONV_W, CONV_CH), CONV_W ** -0.5),
        'a_log': jnp.log(jax.random.uniform(ks[11], (DEPTH, H_A), F32, 1.0, 16.0)),
        'dt_bias': dt + jnp.log(-jnp.expm1(-dt)),
        'onorm_a_w': 1.0 + nrm(ks[12], (DEPTH, DV_A), 0.02),
        'lb_logits': nrm(ks[13], (DEPTH, H_B * DK_B), 0.1),
        'onorm_b_w': 1.0 + nrm(ks[14], (DEPTH, DV_B), 0.02),
        'onorm_c_w': 1.0 + nrm(ks[15], (DEPTH, HD_C), 0.02),
        'w_out': nrm(ks[16], (DEPTH, MIX_WIDTH, D_MODEL), MIX_WIDTH ** -0.5),
        'norm2_w': 1.0 + nrm(ks[17], (DEPTH, D_MODEL), 0.02),
        'peer_wq': nrm(ks[18], (DEPTH, D_MODEL, N_PH * D_QUERY), D_MODEL ** -0.5),
        'peer_keys': nrm(ks[19], (DEPTH, N_PH, 2, N_KEYS, D_QUERY // 2), (D_QUERY // 2) ** -0.5),
        'peer_u': nrm(ks[20], (DEPTH, N_EXPERTS, D_MODEL), D_MODEL ** -0.5),
        'peer_v': nrm(ks[21], (DEPTH, N_EXPERTS, D_MODEL), N_PH ** -0.5),
        'final_norm_w': 1.0 + nrm(ks[22], (D_MODEL,), 0.02),
    }


def reference(x_prompt, x_sample, state_conv_a, state_delta, state_hgrn, cache_swa_k, cache_swa_v,
              norm1_w, w_in, conv_a_w, a_log, dt_bias, onorm_a_w, lb_logits, onorm_b_w, onorm_c_w,
              w_out, norm2_w, peer_wq, peer_keys, peer_u, peer_v, final_norm_w):
    sm = jax.nn.softmax(lb_logits.astype(F32), axis=0)
    lower_bounds = jnp.clip(jnp.cumsum(sm, axis=0) - sm[:1], 0.0, 1.0 - 1e-6)
    bp = x_prompt.shape[0]
    xp, xs = x_prompt, x_sample
    conv_p, conv_s, delta_p, delta_s, hgrn_p, hgrn_s = [], [], [], [], [], []
    kp, vp, ksl, vsl = [], [], [], []
    for li in range(DEPTH):
        weights = (norm1_w[li], w_in[li], conv_a_w[li], a_log[li], dt_bias[li], onorm_a_w[li],
                   lower_bounds[li], onorm_b_w[li], onorm_c_w[li], w_out[li], norm2_w[li],
                   peer_wq[li], peer_keys[li], peer_u[li], peer_v[li])
        xp, c1, d1, h1, k1, v1 = trunk_layer(
            xp, jnp.zeros((bp, CONV_W - 1, CONV_CH), xp.dtype), jnp.zeros((bp, H_A, DK_A, DV_A), F32),
            jnp.zeros((bp, H_B, DK_B, DV_B), F32), jnp.zeros((bp, 0, H_C, HD_C), xp.dtype),
            jnp.zeros((bp, 0, H_C, HD_C), xp.dtype), *weights)
        xs, c2, d2, h2, k2, v2 = trunk_layer(
            xs, state_conv_a[li], state_delta[li], state_hgrn[li], cache_swa_k[li], cache_swa_v[li], *weights)
        conv_p.append(c1); delta_p.append(d1); hgrn_p.append(h1); kp.append(k1); vp.append(v1)
        conv_s.append(c2); delta_s.append(d2); hgrn_s.append(h2); ksl.append(k2); vsl.append(v2)
    y_prompt = rmsnorm(xp, final_norm_w)
    y_sample = rmsnorm(xs, final_norm_w)
    return (y_prompt, y_sample,
            jnp.stack(conv_p), jnp.stack(conv_s),
            jnp.stack(delta_p), jnp.stack(delta_s),
            jnp.stack(hgrn_p), jnp.stack(hgrn_s),
            jnp.stack(kp), jnp.stack(vp),
            jnp.stack(ksl), jnp.stack(vsl))
```

```python
import functools
import math

import numpy as np
import jax
import jax.numpy as jnp
from jax import lax
from jax.experimental import pallas as pl
from jax.experimental.pallas import tpu as pltpu

F32 = jnp.float32
BF16 = jnp.bfloat16

D_MODEL = 1024
H_A, DK_A, DV_A = 4, 128, 128
CONV_W = 4
CHUNK_A = 64
H_B, DK_B, DV_B = 4, 128, 64
CHUNK_B = 16
H_C, HD_C = 4, 64
DIL_GROUPS = ((128, 1), (512, 4), (2048, 16))
MAX_WINDOW = 2048
Q_BLOCK = 128
N_KEYS = 128
N_EXPERTS = N_KEYS * N_KEYS
N_PH = 8
D_QUERY = 256
TOPK = 16
EPS = 1e-6

CONV_CH = 2 * H_A * DK_A + H_A * DV_A
IN_SIZES = (CONV_CH, H_A * DV_A, H_A, H_A, H_B * DK_B, H_B * DK_B, H_B * DV_B, H_B * DV_B, 3 * H_C * HD_C)
IN_OFFS = tuple(int(v) for v in np.cumsum((0,) + IN_SIZES)[:-1])
P_QKV_A, P_Z, P_BQ, P_BF, P_CQKV, P_BI, P_BG, P_BA = 0, 1536, 2048, 2560, 3072, 3840, 4096, 4352
IN_WIDTH_PAD = 4480

LANES = 128
SUBLANES = 8
TOKEN_TILE = 640
EXPERT_TILE = 512
VMEM_LIMIT = 56 * 1024 * 1024


def _in_proj_perm():
    perm = -np.ones((IN_WIDTH_PAD,), np.int64)
    qkv, z, b, a, bq, bf, bi, bg, c = range(9)
    for dst, grp in ((P_QKV_A, qkv), (P_Z, z), (P_BQ, bq), (P_BF, bf), (P_CQKV, c), (P_BI, bi), (P_BG, bg),
                     (P_BA, b), (P_BA + H_A, a)):
        perm[dst:dst + IN_SIZES[grp]] = np.arange(IN_OFFS[grp], IN_OFFS[grp] + IN_SIZES[grp])
    return perm


def _norm_matmul_kernel(x_ref, nw_ref, w_ref, o_ref, h_scr):
    @pl.when(pl.program_id(1) == 0)
    def _():
        x = x_ref[...]
        ms = jnp.mean(x * x, axis=-1, keepdims=True)
        h_scr[...] = (x * lax.rsqrt(ms + EPS) * nw_ref[...]).astype(BF16)

    o_ref[...] = jnp.dot(h_scr[...], w_ref[...], preferred_element_type=F32)


def _norm_matmul(x, norm_w, w_bf16, tn):
    t, d = x.shape
    n = w_bf16.shape[1]
    tm = TOKEN_TILE
    return pl.pallas_call(
        _norm_matmul_kernel,
        grid=(t // tm, n // tn),
        in_specs=[pl.BlockSpec((tm, d), lambda i, j: (i, 0)),
                  pl.BlockSpec((1, d), lambda i, j: (0, 0)),
                  pl.BlockSpec((d, tn), lambda i, j: (0, j))],
        out_specs=pl.BlockSpec((tm, tn), lambda i, j: (i, j)),
        out_shape=jax.ShapeDtypeStruct((t, n), F32),
        scratch_shapes=[pltpu.VMEM((tm, d), BF16)],
        compiler_params=pltpu.CompilerParams(dimension_semantics=("arbitrary", "arbitrary"),
                                             vmem_limit_bytes=VMEM_LIMIT),
        name="norm_in_proj",
    )(x, norm_w.reshape(1, d), w_bf16)


def _out_proj_kernel(x_ref, m_ref, w_ref, o_ref):
    o_ref[...] = x_ref[...] + jnp.dot(m_ref[...].astype(BF16), w_ref[...], preferred_element_type=F32)


def _out_proj(x, mix, w_bf16):
    t, d = x.shape
    tm = TOKEN_TILE
    return pl.pallas_call(
        _out_proj_kernel,
        grid=(t // tm,),
        in_specs=[pl.BlockSpec((tm, d), lambda i: (i, 0)),
                  pl.BlockSpec((tm, d), lambda i: (i, 0)),
                  pl.BlockSpec((d, d), lambda i: (0, 0))],
        out_specs=pl.BlockSpec((tm, d), lambda i: (i, 0)),
        out_shape=jax.ShapeDtypeStruct((t, d), F32),
        compiler_params=pltpu.CompilerParams(dimension_semantics=("arbitrary",), vmem_limit_bytes=VMEM_LIMIT),
        name="out_proj",
    )(x, mix, w_bf16)


def _rmsnorm_kernel(x_ref, nw_ref, o_ref):
    x = x_ref[...]
    ms = jnp.mean(x * x, axis=-1, keepdims=True)
    o_ref[...] = x * lax.rsqrt(ms + EPS) * nw_ref[...]


def _final_norm(x, norm_w):
    t, d = x.shape
    tm = TOKEN_TILE
    return pl.pallas_call(
        _rmsnorm_kernel,
        grid=(t // tm,),
        in_specs=[pl.BlockSpec((tm, d), lambda i: (i, 0)), pl.BlockSpec((1, d), lambda i: (0, 0))],
        out_specs=pl.BlockSpec((tm, d), lambda i: (i, 0)),
        out_shape=jax.ShapeDtypeStruct((t, d), F32),
        compiler_params=pltpu.CompilerParams(dimension_semantics=("arbitrary",)),
        name="final_norm",
    )(x, norm_w.reshape(1, d))


def _sorted_top(s, k):
    rows_f = lax.broadcasted_iota(jnp.int32, s.shape, 0).astype(F32)
    big = float(s.shape[0])
    out = []
    for _ in range(k):
        m = jnp.max(s, axis=0, keepdims=True)
        out.append(m)
        first = jnp.min(jnp.where(s == m, rows_f, big), axis=0, keepdims=True)
        s = jnp.where(rows_f == first, -jnp.inf, s)
    return out


def _route_kernel(x_ref, nw_ref, wqT_ref, keys_ref, h2T_ref, a_ref, b_ref, e1_ref, w_ref, th_ref, qT_scr):
    h = pl.program_id(1)
    n_chunks = x_ref.shape[0] // LANES

    @pl.when(h == 0)
    def _():
        x = x_ref[...]
        ms = jnp.mean(x * x, axis=-1, keepdims=True)
        h2 = x * lax.rsqrt(ms + EPS) * nw_ref[...]
        h2T = h2.T.astype(BF16)
        h2T_ref[...] = h2T
        qT_scr[...] = jnp.dot(wqT_ref[...], h2T, preferred_element_type=F32)

    row0 = pl.multiple_of(h * D_QUERY, D_QUERY)
    half = D_QUERY // 2
    q0 = qT_scr[pl.ds(row0, half), :].astype(BF16)
    q1 = qT_scr[pl.ds(row0 + half, half), :].astype(BF16)
    a_ref[0] = jnp.dot(keys_ref[0, 0], q0, preferred_element_type=F32)
    b_ref[0] = jnp.dot(keys_ref[0, 1], q1, preferred_element_type=F32)

    def chunk(c, carry):
        cs = pl.ds(pl.multiple_of(c * LANES, LANES), LANES)
        s0 = a_ref[0, :, cs]
        s1 = b_ref[0, :, cs]
        a = _sorted_top(s0, TOPK)
        b = _sorted_top(s1, TOPK)
        b_all = jnp.concatenate(b, axis=0)
        b_half = jnp.concatenate(b[:8], axis=0)
        a_tail = jnp.concatenate(a[8:], axis=0)
        cand = jnp.concatenate([b_all + a[0]] + [b_half + a[r] for r in range(1, 8)] + [a_tail + b[0]], axis=0)
        theta = _sorted_top(cand, TOPK)[-1]
        top = a[0] + b[0]
        z = jnp.sum(jnp.where(cand >= theta, jnp.exp(cand - top), 0.0), axis=0, keepdims=True)
        w_ref[0, :, cs] = jnp.exp(s0 - a[0]) / z
        e1_ref[0, :, cs] = jnp.exp(s1 - b[0])
        th_ref[0, :, cs] = theta
        return carry

    lax.fori_loop(0, n_chunks, chunk, 0)


def _peer_route(x, norm_w, wqT, keys_bf16):
    t, d = x.shape
    tm = TOKEN_TILE
    nq = wqT.shape[0]
    per_head = jax.ShapeDtypeStruct((N_PH, N_KEYS, t), F32)
    head_spec = pl.BlockSpec((1, N_KEYS, tm), lambda i, h: (h, 0, i))
    return pl.pallas_call(
        _route_kernel,
        grid=(t // tm, N_PH),
        in_specs=[pl.BlockSpec((tm, d), lambda i, h: (i, 0)),
                  pl.BlockSpec((1, d), lambda i, h: (0, 0)),
                  pl.BlockSpec((nq, d), lambda i, h: (0, 0)),
                  pl.BlockSpec((1, 2, N_KEYS, D_QUERY // 2), lambda i, h: (h, 0, 0, 0))],
        out_specs=[pl.BlockSpec((d, tm), lambda i, h: (0, i)),
                   head_spec, head_spec, head_spec, head_spec,
                   pl.BlockSpec((1, 1, tm), lambda i, h: (h, 0, i))],
        out_shape=[jax.ShapeDtypeStruct((d, t), BF16), per_head, per_head, per_head, per_head,
                   jax.ShapeDtypeStruct((N_PH, 1, t), F32)],
        scratch_shapes=[pltpu.VMEM((nq, tm), F32)],
        compiler_params=pltpu.CompilerParams(dimension_semantics=("arbitrary", "arbitrary"),
                                             vmem_limit_bytes=VMEM_LIMIT),
        name="peer_route",
    )(x, norm_w.reshape(1, d), wqT, keys_bf16)


def _gelu_exact(x):
    return 0.5 * x * (1.0 + lax.erf(x * (1.0 / math.sqrt(2.0))))


def _dense_kernel(h2T_ref, a_ref, w_ref, b_ref, e1_ref, th_ref, u_ref, vT_ref, x_ref, o_ref,
                  acc_scr, act_scr, p_scr):
    e = pl.program_id(1)
    n_chunks = h2T_ref.shape[1] // LANES
    rows_per_step = u_ref.shape[0] // N_KEYS

    @pl.when(e == 0)
    def _():
        acc_scr[...] = jnp.zeros_like(acc_scr)

    act_scr[...] = jnp.dot(u_ref[...], h2T_ref[...], preferred_element_type=F32)

    group = pl.multiple_of((e // 2) * SUBLANES, SUBLANES)
    upper = (e % 2) == 1

    def chunk(c, carry):
        cs = pl.ds(pl.multiple_of(c * LANES, LANES), LANES)
        a_rows, w_rows = [], []
        for h in range(N_PH):
            a8 = a_ref[h, pl.ds(group, SUBLANES), cs]
            w8 = w_ref[h, pl.ds(group, SUBLANES), cs]
            a_rows.append(jnp.where(upper, a8[rows_per_step:], a8[:rows_per_step]))
            w_rows.append(jnp.where(upper, w8[rows_per_step:], w8[:rows_per_step]))
        for ii in range(rows_per_step):
            gate = jnp.zeros((N_KEYS, LANES), F32)
            for h in range(N_PH):
                pair = b_ref[h, :, cs] + a_rows[h][ii:ii + 1]
                val = e1_ref[h, :, cs] * w_rows[h][ii:ii + 1]
                gate = gate + jnp.where(pair >= th_ref[h, :, cs], val, 0.0)
            act = act_scr[ii * N_KEYS:(ii + 1) * N_KEYS, cs]
            p_scr[ii * N_KEYS:(ii + 1) * N_KEYS, cs] = (gate * _gelu_exact(act)).astype(BF16)
        return carry

    lax.fori_loop(0, n_chunks, chunk, 0)
    acc_scr[...] += jnp.dot(vT_ref[...], p_scr[...], preferred_element_type=F32)

    @pl.when(e == pl.num_programs(1) - 1)
    def _():
        o_ref[...] = x_ref[...] + acc_scr[...].T


def _peer_dense(x, h2T, a, w, b, e1, th, u_bf16, vT_bf16):
    t, d = x.shape
    tm, te = TOKEN_TILE, EXPERT_TILE
    head_spec = pl.BlockSpec((N_PH, N_KEYS, tm), lambda i, e: (0, 0, i))
    return pl.pallas_call(
        _dense_kernel,
        grid=(t // tm, N_EXPERTS // te),
        in_specs=[pl.BlockSpec((d, tm), lambda i, e: (0, i)),
                  head_spec, head_spec, head_spec, head_spec,
                  pl.BlockSpec((N_PH, 1, tm), lambda i, e: (0, 0, i)),
                  pl.BlockSpec((te, d), lambda i, e: (e, 0)),
                  pl.BlockSpec((d, te), lambda i, e: (0, e)),
                  pl.BlockSpec((tm, d), lambda i, e: (i, 0))],
        out_specs=pl.BlockSpec((tm, d), lambda i, e: (i, 0)),
        out_shape=jax.ShapeDtypeStruct((t, d), F32),
        scratch_shapes=[pltpu.VMEM((d, tm), F32), pltpu.VMEM((te, tm), F32), pltpu.VMEM((te, tm), BF16)],
        compiler_params=pltpu.CompilerParams(dimension_semantics=("arbitrary", "arbitrary"),
                                             vmem_limit_bytes=VMEM_LIMIT),
        name="peer_dense",
    )(h2T, a, w, b, e1, th, u_bf16, vT_bf16, x)


def _peer_block(x, norm_w, wq, keys, u, v):
    h2T, a, b, e1, w, th = _peer_route(x, norm_w, wq.T.astype(BF16), keys.astype(BF16))
    return _peer_dense(x, h2T, a, w, b, e1, th, u.astype(BF16), v.T.astype(BF16))


def _rmsnorm(x, w):
    return x * lax.rsqrt(jnp.mean(x * x, axis=-1, keepdims=True) + EPS) * w


def _l2norm(x):
    return x * lax.rsqrt(jnp.sum(x * x, axis=-1, keepdims=True) + EPS)


def _pad_seq(t, pad):
    return jnp.pad(t, [(0, 0), (0, pad)] + [(0, 0)] * (t.ndim - 2))


def _to_chunks(t, c):
    b, lp = t.shape[:2]
    t = t.reshape((b, lp // c, c) + t.shape[2:])
    return jnp.moveaxis(t, (1, 3), (0, 2))


def _from_chunks(t):
    t = jnp.moveaxis(t, (0, 2), (1, 3))
    b, n, c = t.shape[:3]
    return t.reshape((b, n * c) + t.shape[3:])


def _causal_conv(x, buf, w):
    L = x.shape[1]
    xp = jnp.concatenate([buf, x], axis=1)
    y = xp[:, 0:L] * w[0]
    for j in range(1, CONV_W):
        y = y + xp[:, j:j + L] * w[j]
    return y, xp[:, -(CONV_W - 1):]


def _gated_delta_rule(q, k, v, beta, g, S0):
    B, L, H, dk = q.shape
    c = min(CHUNK_A, L)
    pad = (-L) % c
    q, k, v, beta, g = [_pad_seq(t, pad) for t in (q * dk ** -0.5, k, v, beta, g)]
    qc, kc, vc = _to_chunks(q, c), _to_chunks(k, c), _to_chunks(v, c)
    bc = _to_chunks(beta, c)
    gcum = jnp.cumsum(_to_chunks(g, c), axis=-1)
    strict = jnp.tril(jnp.ones((c, c), bool), -1)
    incl = jnp.tril(jnp.ones((c, c), bool))
    diff = gcum[..., :, None] - gcum[..., None, :]
    d_strict = jnp.exp(jnp.where(strict, diff, -jnp.inf))
    d_incl = jnp.exp(jnp.where(incl, diff, -jnp.inf))
    kb = kc * bc[..., None]
    a_mat = jnp.eye(c, dtype=F32) + jnp.einsum('nbhtk,nbhsk->nbhts', kb, kc) * d_strict
    u_c = lax.linalg.triangular_solve(a_mat, vc * bc[..., None], left_side=True, lower=True, unit_diagonal=True)
    w_c = lax.linalg.triangular_solve(a_mat, kb * jnp.exp(gcum)[..., None], left_side=True, lower=True,
                                      unit_diagonal=True)
    att = jnp.einsum('nbhtk,nbhsk->nbhts', qc, kc) * d_incl
    qg = qc * jnp.exp(gcum)[..., None]
    kg = kc * jnp.exp(gcum[..., -1:] - gcum)[..., None]
    glast = jnp.exp(gcum[..., -1])

    def step(S, xs):
        u_, w_, att_, qg_, kg_, gl_ = xs
        v_new = u_ - jnp.einsum('bhtk,bhkv->bhtv', w_, S)
        o = jnp.einsum('bhtk,bhkv->bhtv', qg_, S) + jnp.einsum('bhts,bhsv->bhtv', att_, v_new)
        S = S * gl_[..., None, None] + jnp.einsum('bhsk,bhsv->bhkv', kg_, v_new)
        return S, o

    S, o = lax.scan(step, S0, (u_c, w_c, att, qg, kg, glast))
    return _from_chunks(o)[:, :L], S


def _hgrn2_recurrence(q, k, v, logf, S0):
    B, L, H, dk = q.shape
    c = min(CHUNK_B, L)
    pad = (-L) % c
    q, k, v, logf = [_pad_seq(t, pad) for t in (q * dk ** -0.5, k, v, logf)]
    qc, kc, vc = _to_chunks(q, c), _to_chunks(k, c), _to_chunks(v, c)
    gc = jnp.cumsum(_to_chunks(logf, c), axis=3)
    tri = jnp.tril(jnp.ones((c, c), bool))[:, :, None]

    def step(S, xs):
        q_, k_, v_, g_ = xs
        decay = jnp.exp(jnp.where(tri, g_[:, :, :, None, :] - g_[:, :, None, :, :], -jnp.inf))
        att = jnp.einsum('bhtk,bhsk,bhtsk->bhts', q_, k_, decay)
        o = jnp.einsum('bhtk,bhkv->bhtv', q_ * jnp.exp(g_), S) + jnp.einsum('bhts,bhsv->bhtv', att, v_)
        g_last = g_[:, :, -1:, :]
        S = S * jnp.exp(g_last)[:, :, 0, :, None] + jnp.einsum('bhsk,bhsv->bhkv', k_ * jnp.exp(g_last - g_), v_)
        return S, o

    S, o = lax.scan(step, S0, (qc, kc, vc, gc))
    return _from_chunks(o)[:, :L], S


def _dilated_window_attention(q, k, v, q_start):
    B, Lq, H, hd = q.shape
    Lk = k.shape[1]
    blk = min(Q_BLOCK, Lq)
    pad = (-Lq) % blk
    nb = (Lq + pad) // blk
    qb = jnp.moveaxis(_pad_seq(q, pad).reshape(B, nb, blk, H, hd), 1, 0)
    starts = q_start + jnp.arange(nb, dtype=jnp.int32) * blk
    scale = hd ** -0.5

    def block(args):
        qblk, s0 = args
        qpos = s0 + jnp.arange(blk, dtype=jnp.int32)
        outs, lses = [], []
        for window, dil in DIL_GROUPS:
            idx = qpos[:, None] - dil * jnp.arange(window // dil + 1, dtype=jnp.int32)[None, :]
            valid = idx >= 0
            idx = jnp.clip(idx, 0, Lk - 1)
            kg = k[:, idx]
            vg = v[:, idx]
            s = jnp.einsum('bqhd,bqmhd->bqhm', qblk, kg).astype(F32) * scale
            s = jnp.where(valid[None, :, None, :], s, -jnp.inf)
            mx = jnp.max(s, axis=-1, keepdims=True)
            e = jnp.exp(s - mx)
            den = jnp.sum(e, axis=-1, keepdims=True)
            outs.append(jnp.einsum('bqhm,bqmhd->bqhd', e / den, vg.astype(F32)))
            lses.append((mx + jnp.log(den))[..., 0])
        wts = jax.nn.softmax(jnp.stack(lses, axis=-1), axis=-1)
        return jnp.einsum('bqhg,gbqhd->bqhd', wts, jnp.stack(outs, axis=0))

    out = lax.map(block, (qb, starts))
    return jnp.moveaxis(out, 0, 1).reshape(B, nb * blk, H, hd)[:, :Lq]


def _mixers(proj, conv_buf, s_delta, s_hgrn, k_buf, v_buf, conv_a_w, a_log, dt_bias, onorm_a_w, lb, onorm_b_w,
            onorm_c_w):
    B, L, _ = proj.shape
    a_qkv = proj[..., P_QKV_A:P_QKV_A + CONV_CH]
    a_z = proj[..., P_Z:P_Z + H_A * DV_A]
    a_b = proj[..., P_BA:P_BA + H_A]
    a_a = proj[..., P_BA + H_A:P_BA + 2 * H_A]
    b_q = proj[..., P_BQ:P_BQ + H_B * DK_B]
    b_f = proj[..., P_BF:P_BF + H_B * DK_B]
    b_i = proj[..., P_BI:P_BI + H_B * DV_B]
    b_g = proj[..., P_BG:P_BG + H_B * DV_B]
    c_qkv = proj[..., P_CQKV:P_CQKV + 3 * H_C * HD_C]

    a_qkv, new_conv = _causal_conv(a_qkv, conv_buf, conv_a_w)
    a_qkv = jax.nn.silu(a_qkv)
    aq, ak, av = jnp.split(a_qkv, [H_A * DK_A, 2 * H_A * DK_A], axis=-1)
    aq = _l2norm(aq.reshape(B, L, H_A, DK_A))
    ak = _l2norm(ak.reshape(B, L, H_A, DK_A))
    av = av.reshape(B, L, H_A, DV_A)
    beta = jax.nn.sigmoid(a_b)
    g = -jnp.exp(a_log) * jax.nn.softplus(a_a + dt_bias)
    o_a, new_delta = _gated_delta_rule(aq, ak, av, beta, g, s_delta)
    o_a = _rmsnorm(o_a, onorm_a_w) * jax.nn.silu(a_z.reshape(B, L, H_A, DV_A))

    z = b_f.reshape(B, L, H_B, DK_B)
    lb = lb.reshape(H_B, DK_B)
    logf = jnp.logaddexp(jnp.log(lb), jnp.log1p(-lb) + jax.nn.log_sigmoid(z))
    bk = (1.0 - lb) * jax.nn.sigmoid(-z)
    o_b, new_hgrn = _hgrn2_recurrence(b_q.reshape(B, L, H_B, DK_B), bk, b_i.reshape(B, L, H_B, DV_B), logf, s_hgrn)
    o_b = _rmsnorm(o_b, onorm_b_w) * jax.nn.silu(b_g.reshape(B, L, H_B, DV_B))

    cqkv = c_qkv.reshape(B, L, 3, H_C, HD_C)
    cq, ck, cv = cqkv[:, :, 0], cqkv[:, :, 1], cqkv[:, :, 2]
    k_all = jnp.concatenate([k_buf, ck], axis=1)
    v_all = jnp.concatenate([v_buf, cv], axis=1)
    o_c = _rmsnorm(_dilated_window_attention(cq, k_all, v_all, k_buf.shape[1]), onorm_c_w)
    keep = min(MAX_WINDOW, k_all.shape[1])
    mix = jnp.concatenate([o_a.reshape(B, L, -1), o_b.reshape(B, L, -1), o_c.reshape(B, L, -1)], axis=-1)
    return mix, new_conv, new_delta, new_hgrn, k_all[:, -keep:], v_all[:, -keep:]


def kernel(x_prompt, x_sample, state_conv_a, state_delta, state_hgrn, cache_swa_k, cache_swa_v, norm1_w, w_in,
           conv_a_w, a_log, dt_bias, onorm_a_w, lb_logits, onorm_b_w, onorm_c_w, w_out, norm2_w, peer_wq, peer_keys,
           peer_u, peer_v, final_norm_w):
    depth = w_in.shape[0]
    bp, lp, d = x_prompt.shape
    bs, ls, _ = x_sample.shape
    tp = bp * lp
    sm = jax.nn.softmax(lb_logits, axis=0)
    lower_bounds = jnp.clip(jnp.cumsum(sm, axis=0) - sm[:1], 0.0, 1.0 - 1e-6)
    perm = _in_proj_perm()
    col_src = jnp.asarray(np.maximum(perm, 0))
    col_ok = jnp.asarray(perm >= 0)

    x = jnp.concatenate([x_prompt.reshape(tp, d), x_sample.reshape(bs * ls, d)], axis=0)
    outs = [[] for _ in range(10)]
    for li in range(depth):
        w_in_p = jnp.where(col_ok[None, :], w_in[li][:, col_src], 0.0).astype(BF16)
        proj = _norm_matmul(x, norm1_w[li], w_in_p, tn=640)
        mixer_w = (conv_a_w[li], a_log[li], dt_bias[li], onorm_a_w[li], lower_bounds[li], onorm_b_w[li],
                   onorm_c_w[li])
        mix_p, c1, d1, h1, k1, v1 = _mixers(
            proj[:tp].reshape(bp, lp, -1), jnp.zeros((bp, CONV_W - 1, CONV_CH), F32),
            jnp.zeros((bp, H_A, DK_A, DV_A), F32), jnp.zeros((bp, H_B, DK_B, DV_B), F32),
            jnp.zeros((bp, 0, H_C, HD_C), F32), jnp.zeros((bp, 0, H_C, HD_C), F32), *mixer_w)
        mix_s, c2, d2, h2, k2, v2 = _mixers(
            proj[tp:].reshape(bs, ls, -1), state_conv_a[li], state_delta[li], state_hgrn[li], cache_swa_k[li],
            cache_swa_v[li], *mixer_w)
        mix = jnp.concatenate([mix_p.reshape(tp, d), mix_s.reshape(bs * ls, d)], axis=0)
        x = _out_proj(x, mix, w_out[li].astype(BF16))
        x = _peer_block(x, norm2_w[li], peer_wq[li], peer_keys[li], peer_u[li], peer_v[li])
        for lst, val in zip(outs, (c1, c2, d1, d2, h1, h2, k1, v1, k2, v2)):
            lst.append(val)
    y = _final_norm(x, final_norm_w)
    return (y[:tp].reshape(bp, lp, d), y[tp:].reshape(bs, ls, d)) + tuple(jnp.stack(o) for o in outs)
```

```python
import functools
import math

import numpy as np
import jax
import jax.numpy as jnp
from jax import lax
from jax.experimental import pallas as pl
from jax.experimental.pallas import tpu as pltpu

F32 = jnp.float32
BF16 = jnp.bfloat16

D_MODEL = 1024
H_A, DK_A, DV_A = 4, 128, 128
CONV_W = 4
CHUNK_A = 64
H_B, DK_B, DV_B = 4, 128, 64
CHUNK_B = 16
H_C, HD_C = 4, 64
DIL_GROUPS = ((128, 1), (512, 4), (2048, 16))
MAX_WINDOW = 2048
Q_BLOCK = 128
N_KEYS = 128
N_EXPERTS = N_KEYS * N_KEYS
N_PH = 8
D_QUERY = 256
TOPK = 16
EPS = 1e-6

CONV_CH = 2 * H_A * DK_A + H_A * DV_A
IN_SIZES = (CONV_CH, H_A * DV_A, H_A, H_A, H_B * DK_B, H_B * DK_B, H_B * DV_B, H_B * DV_B, 3 * H_C * HD_C)
IN_OFFS = tuple(int(v) for v in np.cumsum((0,) + IN_SIZES)[:-1])
P_QKV_A, P_Z, P_BQ, P_BF, P_CQKV, P_BI, P_BG, P_BA = 0, 1536, 2048, 2560, 3072, 3840, 4096, 4352
IN_WIDTH_PAD = 4480

LANES = 128
SUBLANES = 8
TOKEN_TILE = 640
EXPERT_TILE = 512
VMEM_LIMIT = 56 * 1024 * 1024


def _in_proj_perm():
    perm = -np.ones((IN_WIDTH_PAD,), np.int64)
    qkv, z, b, a, bq, bf, bi, bg, c = range(9)
    for dst, grp in ((P_QKV_A, qkv), (P_Z, z), (P_BQ, bq), (P_BF, bf), (P_CQKV, c), (P_BI, bi), (P_BG, bg),
                     (P_BA, b), (P_BA + H_A, a)):
        perm[dst:dst + IN_SIZES[grp]] = np.arange(IN_OFFS[grp], IN_OFFS[grp] + IN_SIZES[grp])
    return perm


def _norm_matmul_kernel(x_ref, nw_ref, w_ref, o_ref, h_scr):
    @pl.when(pl.program_id(1) == 0)
    def _():
        x = x_ref[...]
        ms = jnp.mean(x * x, axis=-1, keepdims=True)
        h_scr[...] = (x * lax.rsqrt(ms + EPS) * nw_ref[...]).astype(BF16)

    o_ref[...] = jnp.dot(h_scr[...], w_ref[...], preferred_element_type=F32)


def _norm_matmul(x, norm_w, w_bf16, tn):
    t, d = x.shape
    n = w_bf16.shape[1]
    tm = TOKEN_TILE
    return pl.pallas_call(
        _norm_matmul_kernel,
        grid=(t // tm, n // tn),
        in_specs=[pl.BlockSpec((tm, d), lambda i, j: (i, 0)),
                  pl.BlockSpec((1, d), lambda i, j: (0, 0)),
                  pl.BlockSpec((d, tn), lambda i, j: (0, j))],
        out_specs=pl.BlockSpec((tm, tn), lambda i, j: (i, j)),
        out_shape=jax.ShapeDtypeStruct((t, n), F32),
        scratch_shapes=[pltpu.VMEM((tm, d), BF16)],
        compiler_params=pltpu.CompilerParams(dimension_semantics=("arbitrary", "arbitrary"),
                                             vmem_limit_bytes=VMEM_LIMIT),
        name="norm_in_proj",
    )(x, norm_w.reshape(1, d), w_bf16)


def _out_proj_kernel(x_ref, m_ref, w_ref, o_ref):
    o_ref[...] = x_ref[...] + jnp.dot(m_ref[...].astype(BF16), w_ref[...], preferred_element_type=F32)


def _out_proj(x, mix, w_bf16):
    t, d = x.shape
    tm = TOKEN_TILE
    return pl.pallas_call(
        _out_proj_kernel,
        grid=(t // tm,),
        in_specs=[pl.BlockSpec((tm, d), lambda i: (i, 0)),
                  pl.BlockSpec((tm, d), lambda i: (i, 0)),
                  pl.BlockSpec((d, d), lambda i: (0, 0))],
        out_specs=pl.BlockSpec((tm, d), lambda i: (i, 0)),
        out_shape=jax.ShapeDtypeStruct((t, d), F32),
        compiler_params=pltpu.CompilerParams(dimension_semantics=("arbitrary",), vmem_limit_bytes=VMEM_LIMIT),
        name="out_proj",
    )(x, mix, w_bf16)


def _rmsnorm_kernel(x_ref, nw_ref, o_ref):
    x = x_ref[...]
    ms = jnp.mean(x * x, axis=-1, keepdims=True)
    o_ref[...] = x * lax.rsqrt(ms + EPS) * nw_ref[...]


def _final_norm(x, norm_w):
    t, d = x.shape
    tm = TOKEN_TILE
    return pl.pallas_call(
        _rmsnorm_kernel,
        grid=(t // tm,),
        in_specs=[pl.BlockSpec((tm, d), lambda i: (i, 0)), pl.BlockSpec((1, d), lambda i: (0, 0))],
        out_specs=pl.BlockSpec((tm, d), lambda i: (i, 0)),
        out_shape=jax.ShapeDtypeStruct((t, d), F32),
        compiler_params=pltpu.CompilerParams(dimension_semantics=("arbitrary",)),
        name="final_norm",
    )(x, norm_w.reshape(1, d))


def _sorted_top(s, k):
    rows_f = lax.broadcasted_iota(jnp.int32, s.shape, 0).astype(F32)
    big = float(s.shape[0])
    out = []
    for _ in range(k):
        m = jnp.max(s, axis=0, keepdims=True)
        out.append(m)
        first = jnp.min(jnp.where(s == m, rows_f, big), axis=0, keepdims=True)
        s = jnp.where(rows_f == first, -jnp.inf, s)
    return out


def _route_kernel(x_ref, nw_ref, wqT_ref, keys_ref, h2T_ref, a_ref, b_ref, e1_ref, w_ref, th_ref, qT_scr):
    h = pl.program_id(1)
    n_chunks = x_ref.shape[0] // LANES

    @pl.when(h == 0)
    def _():
        x = x_ref[...]
        ms = jnp.mean(x * x, axis=-1, keepdims=True)
        h2 = x * lax.rsqrt(ms + EPS) * nw_ref[...]
        h2T = h2.T.astype(BF16)
        h2T_ref[...] = h2T
        qT_scr[...] = jnp.dot(wqT_ref[...], h2T, preferred_element_type=F32)

    row0 = pl.multiple_of(h * D_QUERY, D_QUERY)
    half = D_QUERY // 2
    q0 = qT_scr[pl.ds(row0, half), :].astype(BF16)
    q1 = qT_scr[pl.ds(row0 + half, half), :].astype(BF16)
    a_ref[0] = jnp.dot(keys_ref[0, 0], q0, preferred_element_type=F32)
    b_ref[0] = jnp.dot(keys_ref[0, 1], q1, preferred_element_type=F32)

    def chunk(c, carry):
        cs = pl.ds(pl.multiple_of(c * LANES, LANES), LANES)
        s0 = a_ref[0, :, cs]
        s1 = b_ref[0, :, cs]
        a = _sorted_top(s0, TOPK)
        b = _sorted_top(s1, TOPK)
        b_all = jnp.concatenate(b, axis=0)
        b_half = jnp.concatenate(b[:8], axis=0)
        a_tail = jnp.concatenate(a[8:], axis=0)
        cand = jnp.concatenate([b_all + a[0]] + [b_half + a[r] for r in range(1, 8)] + [a_tail + b[0]], axis=0)
        theta = _sorted_top(cand, TOPK)[-1]
        top = a[0] + b[0]
        z = jnp.sum(jnp.where(cand >= theta, jnp.exp(cand - top), 0.0), axis=0, keepdims=True)
        w_ref[0, :, cs] = jnp.exp(s0 - a[0]) / z
        e1_ref[0, :, cs] = jnp.exp(s1 - b[0])
        th_ref[0, :, cs] = theta
        return carry

    lax.fori_loop(0, n_chunks, chunk, 0)


def _peer_route(x, norm_w, wqT, keys_bf16):
    t, d = x.shape
    tm = TOKEN_TILE
    nq = wqT.shape[0]
    per_head = jax.ShapeDtypeStruct((N_PH, N_KEYS, t), F32)
    head_spec = pl.BlockSpec((1, N_KEYS, tm), lambda i, h: (h, 0, i))
    return pl.pallas_call(
        _route_kernel,
        grid=(t // tm, N_PH),
        in_specs=[pl.BlockSpec((tm, d), lambda i, h: (i, 0)),
                  pl.BlockSpec((1, d), lambda i, h: (0, 0)),
                  pl.BlockSpec((nq, d), lambda i, h: (0, 0)),
                  pl.BlockSpec((1, 2, N_KEYS, D_QUERY // 2), lambda i, h: (h, 0, 0, 0))],
        out_specs=[pl.BlockSpec((d, tm), lambda i, h: (0, i)),
                   head_spec, head_spec, head_spec, head_spec,
                   pl.BlockSpec((1, 1, tm), lambda i, h: (h, 0, i))],
        out_shape=[jax.ShapeDtypeStruct((d, t), BF16), per_head, per_head, per_head, per_head,
                   jax.ShapeDtypeStruct((N_PH, 1, t), F32)],
        scratch_shapes=[pltpu.VMEM((nq, tm), F32)],
        compiler_params=pltpu.CompilerParams(dimension_semantics=("arbitrary", "arbitrary"),
                                             vmem_limit_bytes=VMEM_LIMIT),
        name="peer_route",
    )(x, norm_w.reshape(1, d), wqT, keys_bf16)


def _gelu_exact(x):
    return 0.5 * x * (1.0 + lax.erf(x * (1.0 / math.sqrt(2.0))))


def _dense_kernel(h2T_ref, a_ref, w_ref, b_ref, e1_ref, th_ref, u_ref, vT_ref, x_ref, o_ref,
                  acc_scr, act_scr, p_scr):
    e = pl.program_id(1)
    n_chunks = h2T_ref.shape[1] // LANES
    rows_per_step = u_ref.shape[0] // N_KEYS

    @pl.when(e == 0)
    def _():
        acc_scr[...] = jnp.zeros_like(acc_scr)

    act_scr[...] = jnp.dot(u_ref[...], h2T_ref[...], preferred_element_type=F32)

    group = pl.multiple_of((e // 2) * SUBLANES, SUBLANES)
    upper = (e % 2) == 1

    def chunk(c, carry):
        cs = pl.ds(pl.multiple_of(c * LANES, LANES), LANES)
        a_rows, w_rows = [], []
        for h in range(N_PH):
            a8 = a_ref[h, pl.ds(group, SUBLANES), cs]
            w8 = w_ref[h, pl.ds(group, SUBLANES), cs]
            a_rows.append(jnp.where(upper, a8[rows_per_step:], a8[:rows_per_step]))
            w_rows.append(jnp.where(upper, w8[rows_per_step:], w8[:rows_per_step]))
        for ii in range(rows_per_step):
            gate = jnp.zeros((N_KEYS, LANES), F32)
            for h in range(N_PH):
                pair = b_ref[h, :, cs] + a_rows[h][ii:ii + 1]
                val = e1_ref[h, :, cs] * w_rows[h][ii:ii + 1]
                gate = gate + jnp.where(pair >= th_ref[h, :, cs], val, 0.0)
            act = act_scr[ii * N_KEYS:(ii + 1) * N_KEYS, cs]
            p_scr[ii * N_KEYS:(ii + 1) * N_KEYS, cs] = (gate * _gelu_exact(act)).astype(BF16)
        return carry

    lax.fori_loop(0, n_chunks, chunk, 0)
    acc_scr[...] += jnp.dot(vT_ref[...], p_scr[...], preferred_element_type=F32)

    @pl.when(e == pl.num_programs(1) - 1)
    def _():
        o_ref[...] = x_ref[...] + acc_scr[...].T


def _peer_dense(x, h2T, a, w, b, e1, th, u_bf16, vT_bf16):
    t, d = x.shape
    tm, te = TOKEN_TILE, EXPERT_TILE
    head_spec = pl.BlockSpec((N_PH, N_KEYS, tm), lambda i, e: (0, 0, i))
    return pl.pallas_call(
        _dense_kernel,
        grid=(t // tm, N_EXPERTS // te),
        in_specs=[pl.BlockSpec((d, tm), lambda i, e: (0, i)),
                  head_spec, head_spec, head_spec, head_spec,
                  pl.BlockSpec((N_PH, 1, tm), lambda i, e: (0, 0, i)),
                  pl.BlockSpec((te, d), lambda i, e: (e, 0)),
                  pl.BlockSpec((d, te), lambda i, e: (0, e)),
                  pl.BlockSpec((tm, d), lambda i, e: (i, 0))],
        out_specs=pl.BlockSpec((tm, d), lambda i, e: (i, 0)),
        out_shape=jax.ShapeDtypeStruct((t, d), F32),
        scratch_shapes=[pltpu.VMEM((d, tm), F32), pltpu.VMEM((te, tm), F32), pltpu.VMEM((te, tm), BF16)],
        compiler_params=pltpu.CompilerParams(dimension_semantics=("arbitrary", "arbitrary"),
                                             vmem_limit_bytes=VMEM_LIMIT),
        name="peer_dense",
    )(h2T, a, w, b, e1, th, u_bf16, vT_bf16, x)


def _peer_block(x, norm_w, wq, keys, u, v):
    h2T, a, b, e1, w, th = _peer_route(x, norm_w, wq.T.astype(BF16), keys.astype(BF16))
    return _peer_dense(x, h2T, a, w, b, e1, th, u.astype(BF16), v.T.astype(BF16))


ATT_TQ = 256
ATT_WIN = MAX_WINDOW + ATT_TQ
ATT_W = H_C * HD_C


def _log_multiplicity(dist):
    count = jnp.zeros(dist.shape, F32)
    for window, dil in DIL_GROUPS:
        count = count + ((dist >= 0) & (dist <= window) & (dist % dil == 0)).astype(F32)
    return jnp.where(count > 0, jnp.log(jnp.maximum(count, 1.0)), -jnp.inf)


def _head_norm_store(acc, nw_ref, o_ref):
    lane = lax.broadcasted_iota(jnp.int32, (1, ATT_W), 1)
    sq = acc * acc
    inv = jnp.zeros_like(acc)
    for h in range(H_C):
        hm = (lane >= h * HD_C) & (lane < (h + 1) * HD_C)
        ms = jnp.sum(jnp.where(hm, sq, 0.0), axis=-1, keepdims=True) * (1.0 / HD_C)
        inv = inv + jnp.where(hm, lax.rsqrt(ms + EPS), 0.0)
    o_ref[...] = acc * inv * nw_ref[...]


def _attn_prompt_kernel(q_ref, k_ref, v_ref, bias_ref, nw_ref, o_ref):
    i = pl.program_id(1)
    tq = q_ref.shape[0]
    lk = k_ref.shape[0]
    w0 = pl.multiple_of(jnp.clip(i * tq - MAX_WINDOW, 0, lk - ATT_WIN), tq)
    kw = k_ref[pl.ds(w0, ATT_WIN), :]
    vw = v_ref[pl.ds(w0, ATT_WIN), :]
    q = q_ref[...] * (HD_C ** -0.5)
    lane = lax.broadcasted_iota(jnp.int32, (1, ATT_W), 1)
    acc = jnp.zeros((tq, ATT_W), F32)
    for h in range(H_C):
        hm = (lane >= h * HD_C) & (lane < (h + 1) * HD_C)
        qh = jnp.where(hm, q, 0.0).astype(BF16)
        s = lax.dot_general(qh, kw, (((1,), (1,)), ((), ())), preferred_element_type=F32) + bias_ref[0]
        m = jnp.max(s, axis=-1, keepdims=True)
        p = jnp.exp(s - m)
        den = jnp.sum(p, axis=-1, keepdims=True)
        oh = jnp.dot(p.astype(BF16), vw, preferred_element_type=F32) / den
        acc = acc + jnp.where(hm, oh, 0.0)
    _head_norm_store(acc, nw_ref, o_ref)


def _attn_prompt(proj, kv_bf16, onorm_c_w, batch, seq):
    nblk = seq // ATT_TQ
    n_bias = MAX_WINDOW // ATT_TQ + 1
    delta = jnp.arange(n_bias, dtype=jnp.int32)[:, None, None] * ATT_TQ
    dist = (delta + jnp.arange(ATT_TQ, dtype=jnp.int32)[None, :, None]
            - jnp.arange(ATT_WIN, dtype=jnp.int32)[None, None, :])
    bias = _log_multiplicity(dist)
    qcol = P_CQKV // ATT_W
    return pl.pallas_call(
        _attn_prompt_kernel,
        grid=(batch, nblk),
        in_specs=[pl.BlockSpec((ATT_TQ, ATT_W), lambda b, i: (b * nblk + i, qcol)),
                  pl.BlockSpec((seq, ATT_W), lambda b, i: (b, 0)),
                  pl.BlockSpec((seq, ATT_W), lambda b, i: (b, 1)),
                  pl.BlockSpec((1, ATT_TQ, ATT_WIN), lambda b, i: (jnp.minimum(i, n_bias - 1), 0, 0)),
                  pl.BlockSpec((1, ATT_W), lambda b, i: (0, 0))],
        out_specs=pl.BlockSpec((ATT_TQ, ATT_W), lambda b, i: (b * nblk + i, 0)),
        out_shape=jax.ShapeDtypeStruct((batch * seq, ATT_W), F32),
        compiler_params=pltpu.CompilerParams(dimension_semantics=("arbitrary", "arbitrary"),
                                             vmem_limit_bytes=VMEM_LIMIT),
        name="attn_prompt",
    )(proj, kv_bf16, kv_bf16, bias, jnp.tile(onorm_c_w, H_C).reshape(1, ATT_W))


def _rmsnorm(x, w):
    return x * lax.rsqrt(jnp.mean(x * x, axis=-1, keepdims=True) + EPS) * w


def _l2norm(x):
    return x * lax.rsqrt(jnp.sum(x * x, axis=-1, keepdims=True) + EPS)


def _pad_seq(t, pad):
    return jnp.pad(t, [(0, 0), (0, pad)] + [(0, 0)] * (t.ndim - 2))


def _to_chunks(t, c):
    b, lp = t.shape[:2]
    t = t.reshape((b, lp // c, c) + t.shape[2:])
    return jnp.moveaxis(t, (1, 3), (0, 2))


def _from_chunks(t):
    t = jnp.moveaxis(t, (0, 2), (1, 3))
    b, n, c = t.shape[:3]
    return t.reshape((b, n * c) + t.shape[3:])


def _causal_conv(x, buf, w):
    L = x.shape[1]
    xp = jnp.concatenate([buf, x], axis=1)
    y = xp[:, 0:L] * w[0]
    for j in range(1, CONV_W):
        y = y + xp[:, j:j + L] * w[j]
    return y, xp[:, -(CONV_W - 1):]


def _gated_delta_rule(q, k, v, beta, g, S0):
    B, L, H, dk = q.shape
    c = min(CHUNK_A, L)
    pad = (-L) % c
    q, k, v, beta, g = [_pad_seq(t, pad) for t in (q * dk ** -0.5, k, v, beta, g)]
    qc, kc, vc = _to_chunks(q, c), _to_chunks(k, c), _to_chunks(v, c)
    bc = _to_chunks(beta, c)
    gcum = jnp.cumsum(_to_chunks(g, c), axis=-1)
    strict = jnp.tril(jnp.ones((c, c), bool), -1)
    incl = jnp.tril(jnp.ones((c, c), bool))
    diff = gcum[..., :, None] - gcum[..., None, :]
    d_strict = jnp.exp(jnp.where(strict, diff, -jnp.inf))
    d_incl = jnp.exp(jnp.where(incl, diff, -jnp.inf))
    kb = kc * bc[..., None]
    a_mat = jnp.eye(c, dtype=F32) + jnp.einsum('nbhtk,nbhsk->nbhts', kb, kc) * d_strict
    u_c = lax.linalg.triangular_solve(a_mat, vc * bc[..., None], left_side=True, lower=True, unit_diagonal=True)
    w_c = lax.linalg.triangular_solve(a_mat, kb * jnp.exp(gcum)[..., None], left_side=True, lower=True,
                                      unit_diagonal=True)
    att = jnp.einsum('nbhtk,nbhsk->nbhts', qc, kc) * d_incl
    qg = qc * jnp.exp(gcum)[..., None]
    kg = kc * jnp.exp(gcum[..., -1:] - gcum)[..., None]
    glast = jnp.exp(gcum[..., -1])

    def step(S, xs):
        u_, w_, att_, qg_, kg_, gl_ = xs
        v_new = u_ - jnp.einsum('bhtk,bhkv->bhtv', w_, S)
        o = jnp.einsum('bhtk,bhkv->bhtv', qg_, S) + jnp.einsum('bhts,bhsv->bhtv', att_, v_new)
        S = S * gl_[..., None, None] + jnp.einsum('bhsk,bhsv->bhkv', kg_, v_new)
        return S, o

    S, o = lax.scan(step, S0, (u_c, w_c, att, qg, kg, glast))
    return _from_chunks(o)[:, :L], S


def _hgrn2_recurrence(q, k, v, logf, S0):
    B, L, H, dk = q.shape
    c = min(CHUNK_B, L)
    pad = (-L) % c
    q, k, v, logf = [_pad_seq(t, pad) for t in (q * dk ** -0.5, k, v, logf)]
    qc, kc, vc = _to_chunks(q, c), _to_chunks(k, c), _to_chunks(v, c)
    gc = jnp.cumsum(_to_chunks(logf, c), axis=3)
    tri = jnp.tril(jnp.ones((c, c), bool))[:, :, None]

    def step(S, xs):
        q_, k_, v_, g_ = xs
        decay = jnp.exp(jnp.where(tri, g_[:, :, :, None, :] - g_[:, :, None, :, :], -jnp.inf))
        att = jnp.einsum('bhtk,bhsk,bhtsk->bhts', q_, k_, decay)
        o = jnp.einsum('bhtk,bhkv->bhtv', q_ * jnp.exp(g_), S) + jnp.einsum('bhts,bhsv->bhtv', att, v_)
        g_last = g_[:, :, -1:, :]
        S = S * jnp.exp(g_last)[:, :, 0, :, None] + jnp.einsum('bhsk,bhsv->bhkv', k_ * jnp.exp(g_last - g_), v_)
        return S, o

    S, o = lax.scan(step, S0, (qc, kc, vc, gc))
    return _from_chunks(o)[:, :L], S


def _dilated_window_attention(q, k, v, q_start):
    B, Lq, H, hd = q.shape
    Lk = k.shape[1]
    blk = min(Q_BLOCK, Lq)
    pad = (-Lq) % blk
    nb = (Lq + pad) // blk
    qb = jnp.moveaxis(_pad_seq(q, pad).reshape(B, nb, blk, H, hd), 1, 0)
    starts = q_start + jnp.arange(nb, dtype=jnp.int32) * blk
    scale = hd ** -0.5

    def block(args):
        qblk, s0 = args
        qpos = s0 + jnp.arange(blk, dtype=jnp.int32)
        outs, lses = [], []
        for window, dil in DIL_GROUPS:
            idx = qpos[:, None] - dil * jnp.arange(window // dil + 1, dtype=jnp.int32)[None, :]
            valid = idx >= 0
            idx = jnp.clip(idx, 0, Lk - 1)
            kg = k[:, idx]
            vg = v[:, idx]
            s = jnp.einsum('bqhd,bqmhd->bqhm', qblk, kg).astype(F32) * scale
            s = jnp.where(valid[None, :, None, :], s, -jnp.inf)
            mx = jnp.max(s, axis=-1, keepdims=True)
            e = jnp.exp(s - mx)
            den = jnp.sum(e, axis=-1, keepdims=True)
            outs.append(jnp.einsum('bqhm,bqmhd->bqhd', e / den, vg.astype(F32)))
            lses.append((mx + jnp.log(den))[..., 0])
        wts = jax.nn.softmax(jnp.stack(lses, axis=-1), axis=-1)
        return jnp.einsum('bqhg,gbqhd->bqhd', wts, jnp.stack(outs, axis=0))

    out = lax.map(block, (qb, starts))
    return jnp.moveaxis(out, 0, 1).reshape(B, nb * blk, H, hd)[:, :Lq]


def _mixers(proj, conv_buf, s_delta, s_hgrn, k_buf, v_buf, conv_a_w, a_log, dt_bias, onorm_a_w, lb, onorm_b_w,
            onorm_c_w, o_c=None):
    B, L, _ = proj.shape
    a_qkv = proj[..., P_QKV_A:P_QKV_A + CONV_CH]
    a_z = proj[..., P_Z:P_Z + H_A * DV_A]
    a_b = proj[..., P_BA:P_BA + H_A]
    a_a = proj[..., P_BA + H_A:P_BA + 2 * H_A]
    b_q = proj[..., P_BQ:P_BQ + H_B * DK_B]
    b_f = proj[..., P_BF:P_BF + H_B * DK_B]
    b_i = proj[..., P_BI:P_BI + H_B * DV_B]
    b_g = proj[..., P_BG:P_BG + H_B * DV_B]
    c_qkv = proj[..., P_CQKV:P_CQKV + 3 * H_C * HD_C]

    a_qkv, new_conv = _causal_conv(a_qkv, conv_buf, conv_a_w)
    a_qkv = jax.nn.silu(a_qkv)
    aq, ak, av = jnp.split(a_qkv, [H_A * DK_A, 2 * H_A * DK_A], axis=-1)
    aq = _l2norm(aq.reshape(B, L, H_A, DK_A))
    ak = _l2norm(ak.reshape(B, L, H_A, DK_A))
    av = av.reshape(B, L, H_A, DV_A)
    beta = jax.nn.sigmoid(a_b)
    g = -jnp.exp(a_log) * jax.nn.softplus(a_a + dt_bias)
    o_a, new_delta = _gated_delta_rule(aq, ak, av, beta, g, s_delta)
    o_a = _rmsnorm(o_a, onorm_a_w) * jax.nn.silu(a_z.reshape(B, L, H_A, DV_A))

    z = b_f.reshape(B, L, H_B, DK_B)
    lb = lb.reshape(H_B, DK_B)
    logf = jnp.logaddexp(jnp.log(lb), jnp.log1p(-lb) + jax.nn.log_sigmoid(z))
    bk = (1.0 - lb) * jax.nn.sigmoid(-z)
    o_b, new_hgrn = _hgrn2_recurrence(b_q.reshape(B, L, H_B, DK_B), bk, b_i.reshape(B, L, H_B, DV_B), logf, s_hgrn)
    o_b = _rmsnorm(o_b, onorm_b_w) * jax.nn.silu(b_g.reshape(B, L, H_B, DV_B))

    cqkv = c_qkv.reshape(B, L, 3, H_C, HD_C)
    cq, ck, cv = cqkv[:, :, 0], cqkv[:, :, 1], cqkv[:, :, 2]
    k_all = jnp.concatenate([k_buf, ck], axis=1)
    v_all = jnp.concatenate([v_buf, cv], axis=1)
    if o_c is None:
        o_c = _rmsnorm(_dilated_window_attention(cq, k_all, v_all, k_buf.shape[1]), onorm_c_w)
    keep = min(MAX_WINDOW, k_all.shape[1])
    mix = jnp.concatenate([o_a.reshape(B, L, -1), o_b.reshape(B, L, -1), o_c.reshape(B, L, -1)], axis=-1)
    return mix, new_conv, new_delta, new_hgrn, k_all[:, -keep:], v_all[:, -keep:]


def kernel(x_prompt, x_sample, state_conv_a, state_delta, state_hgrn, cache_swa_k, cache_swa_v, norm1_w, w_in,
           conv_a_w, a_log, dt_bias, onorm_a_w, lb_logits, onorm_b_w, onorm_c_w, w_out, norm2_w, peer_wq, peer_keys,
           peer_u, peer_v, final_norm_w):
    depth = w_in.shape[0]
    bp, lp, d = x_prompt.shape
    bs, ls, _ = x_sample.shape
    tp = bp * lp
    sm = jax.nn.softmax(lb_logits, axis=0)
    lower_bounds = jnp.clip(jnp.cumsum(sm, axis=0) - sm[:1], 0.0, 1.0 - 1e-6)
    perm = _in_proj_perm()
    col_src = jnp.asarray(np.maximum(perm, 0))
    col_ok = jnp.asarray(perm >= 0)

    x = jnp.concatenate([x_prompt.reshape(tp, d), x_sample.reshape(bs * ls, d)], axis=0)
    outs = [[] for _ in range(10)]
    for li in range(depth):
        w_in_p = jnp.where(col_ok[None, :], w_in[li][:, col_src], 0.0).astype(BF16)
        proj = _norm_matmul(x, norm1_w[li], w_in_p, tn=640)
        mixer_w = (conv_a_w[li], a_log[li], dt_bias[li], onorm_a_w[li], lower_bounds[li], onorm_b_w[li],
                   onorm_c_w[li])
        kv_p = proj[:tp, P_CQKV + ATT_W:P_CQKV + 3 * ATT_W].astype(BF16)
        o_c_p = _attn_prompt(proj, kv_p, onorm_c_w[li], bp, lp).reshape(bp, lp, H_C, HD_C)
        mix_p, c1, d1, h1, k1, v1 = _mixers(
            proj[:tp].reshape(bp, lp, -1), jnp.zeros((bp, CONV_W - 1, CONV_CH), F32),
            jnp.zeros((bp, H_A, DK_A, DV_A), F32), jnp.zeros((bp, H_B, DK_B, DV_B), F32),
            jnp.zeros((bp, 0, H_C, HD_C), F32), jnp.zeros((bp, 0, H_C, HD_C), F32), *mixer_w, o_c=o_c_p)
        mix_s, c2, d2, h2, k2, v2 = _mixers(
            proj[tp:].reshape(bs, ls, -1), state_conv_a[li], state_delta[li], state_hgrn[li], cache_swa_k[li],
            cache_swa_v[li], *mixer_w)
        mix = jnp.concatenate([mix_p.reshape(tp, d), mix_s.reshape(bs * ls, d)], axis=0)
        x = _out_proj(x, mix, w_out[li].astype(BF16))
        x = _peer_block(x, norm2_w[li], peer_wq[li], peer_keys[li], peer_u[li], peer_v[li])
        for lst, val in zip(outs, (c1, c2, d1, d2, h1, h2, k1, v1, k2, v2)):
            lst.append(val)
    y = _final_norm(x, final_norm_w)
    return (y[:tp].reshape(bp, lp, d), y[tp:].reshape(bs, ls, d)) + tuple(jnp.stack(o) for o in outs)
```

```python
import functools
import math

import numpy as np
import jax
import jax.numpy as jnp
from jax import lax
from jax.experimental import pallas as pl
from jax.experimental.pallas import tpu as pltpu

F32 = jnp.float32
BF16 = jnp.bfloat16

D_MODEL = 1024
H_A, DK_A, DV_A = 4, 128, 128
CONV_W = 4
CHUNK_A = 64
H_B, DK_B, DV_B = 4, 128, 64
CHUNK_B = 16
H_C, HD_C = 4, 64
DIL_GROUPS = ((128, 1), (512, 4), (2048, 16))
MAX_WINDOW = 2048
Q_BLOCK = 128
N_KEYS = 128
N_EXPERTS = N_KEYS * N_KEYS
N_PH = 8
D_QUERY = 256
TOPK = 16
EPS = 1e-6

CONV_CH = 2 * H_A * DK_A + H_A * DV_A
IN_SIZES = (CONV_CH, H_A * DV_A, H_A, H_A, H_B * DK_B, H_B * DK_B, H_B * DV_B, H_B * DV_B, 3 * H_C * HD_C)
IN_OFFS = tuple(int(v) for v in np.cumsum((0,) + IN_SIZES)[:-1])
P_QKV_A, P_Z, P_BQ, P_BF, P_CQKV, P_BI, P_BG, P_BA = 0, 1536, 2048, 2560, 3072, 3840, 4096, 4352
IN_WIDTH_PAD = 4480

LANES = 128
SUBLANES = 8
TOKEN_TILE = 512
EXPERT_TILE = 512
VMEM_LIMIT = 56 * 1024 * 1024


def _in_proj_perm():
    perm = -np.ones((IN_WIDTH_PAD,), np.int64)
    qkv, z, b, a, bq, bf, bi, bg, c = range(9)
    for dst, grp in ((P_QKV_A, qkv), (P_Z, z), (P_BQ, bq), (P_BF, bf), (P_CQKV, c), (P_BI, bi), (P_BG, bg),
                     (P_BA, b), (P_BA + H_A, a)):
        perm[dst:dst + IN_SIZES[grp]] = np.arange(IN_OFFS[grp], IN_OFFS[grp] + IN_SIZES[grp])
    return perm


def _norm_matmul_kernel(x_ref, nw_ref, w_ref, o_ref, h_scr):
    @pl.when(pl.program_id(1) == 0)
    def _():
        x = x_ref[...]
        ms = jnp.mean(x * x, axis=-1, keepdims=True)
        h_scr[...] = (x * lax.rsqrt(ms + EPS) * nw_ref[...]).astype(BF16)

    o_ref[...] = jnp.dot(h_scr[...], w_ref[...], preferred_element_type=F32)


def _norm_matmul(x, norm_w, w_bf16, tn):
    t, d = x.shape
    n = w_bf16.shape[1]
    tm = TOKEN_TILE
    return pl.pallas_call(
        _norm_matmul_kernel,
        grid=(t // tm, n // tn),
        in_specs=[pl.BlockSpec((tm, d), lambda i, j: (i, 0)),
                  pl.BlockSpec((1, d), lambda i, j: (0, 0)),
                  pl.BlockSpec((d, tn), lambda i, j: (0, j))],
        out_specs=pl.BlockSpec((tm, tn), lambda i, j: (i, j)),
        out_shape=jax.ShapeDtypeStruct((t, n), F32),
        scratch_shapes=[pltpu.VMEM((tm, d), BF16)],
        compiler_params=pltpu.CompilerParams(dimension_semantics=("arbitrary", "arbitrary"),
                                             vmem_limit_bytes=VMEM_LIMIT),
        name="norm_in_proj",
    )(x, norm_w.reshape(1, d), w_bf16)


def _out_proj_kernel(x_ref, m_ref, w_ref, o_ref):
    o_ref[...] = x_ref[...] + jnp.dot(m_ref[...].astype(BF16), w_ref[...], preferred_element_type=F32)


def _out_proj(x, mix, w_bf16):
    t, d = x.shape
    tm = TOKEN_TILE
    return pl.pallas_call(
        _out_proj_kernel,
        grid=(t // tm,),
        in_specs=[pl.BlockSpec((tm, d), lambda i: (i, 0)),
                  pl.BlockSpec((tm, d), lambda i: (i, 0)),
                  pl.BlockSpec((d, d), lambda i: (0, 0))],
        out_specs=pl.BlockSpec((tm, d), lambda i: (i, 0)),
        out_shape=jax.ShapeDtypeStruct((t, d), F32),
        compiler_params=pltpu.CompilerParams(dimension_semantics=("arbitrary",), vmem_limit_bytes=VMEM_LIMIT),
        name="out_proj",
    )(x, mix, w_bf16)


def _rmsnorm_kernel(x_ref, nw_ref, o_ref):
    x = x_ref[...]
    ms = jnp.mean(x * x, axis=-1, keepdims=True)
    o_ref[...] = x * lax.rsqrt(ms + EPS) * nw_ref[...]


def _final_norm(x, norm_w):
    t, d = x.shape
    tm = TOKEN_TILE
    return pl.pallas_call(
        _rmsnorm_kernel,
        grid=(t // tm,),
        in_specs=[pl.BlockSpec((tm, d), lambda i: (i, 0)), pl.BlockSpec((1, d), lambda i: (0, 0))],
        out_specs=pl.BlockSpec((tm, d), lambda i: (i, 0)),
        out_shape=jax.ShapeDtypeStruct((t, d), F32),
        compiler_params=pltpu.CompilerParams(dimension_semantics=("arbitrary",)),
        name="final_norm",
    )(x, norm_w.reshape(1, d))


def _sorted_top(s, k):
    rows_f = lax.broadcasted_iota(jnp.int32, s.shape, 0).astype(F32)
    big = float(s.shape[0])
    out = []
    for _ in range(k):
        m = jnp.max(s, axis=0, keepdims=True)
        out.append(m)
        first = jnp.min(jnp.where(s == m, rows_f, big), axis=0, keepdims=True)
        s = jnp.where(rows_f == first, -jnp.inf, s)
    return out


def _route_kernel(x_ref, nw_ref, wqT_ref, keys_ref, h2T_ref, a_ref, b_ref, e1_ref, w_ref, th_ref, qT_scr):
    h = pl.program_id(1)
    n_chunks = x_ref.shape[0] // LANES

    @pl.when(h == 0)
    def _():
        x = x_ref[...]
        ms = jnp.mean(x * x, axis=-1, keepdims=True)
        h2 = x * lax.rsqrt(ms + EPS) * nw_ref[...]
        h2T = h2.T.astype(BF16)
        h2T_ref[...] = h2T
        qT_scr[...] = jnp.dot(wqT_ref[...], h2T, preferred_element_type=F32)

    row0 = pl.multiple_of(h * D_QUERY, D_QUERY)
    half = D_QUERY // 2
    q0 = qT_scr[pl.ds(row0, half), :].astype(BF16)
    q1 = qT_scr[pl.ds(row0 + half, half), :].astype(BF16)
    a_ref[0] = jnp.dot(keys_ref[0, 0], q0, preferred_element_type=F32)
    b_ref[0] = jnp.dot(keys_ref[0, 1], q1, preferred_element_type=F32)

    def chunk(c, carry):
        cs = pl.ds(pl.multiple_of(c * LANES, LANES), LANES)
        s0 = a_ref[0, :, cs]
        s1 = b_ref[0, :, cs]
        a = _sorted_top(s0, TOPK)
        b = _sorted_top(s1, TOPK)
        b_all = jnp.concatenate(b, axis=0)
        b_half = jnp.concatenate(b[:8], axis=0)
        a_tail = jnp.concatenate(a[8:], axis=0)
        cand = jnp.concatenate([b_all + a[0]] + [b_half + a[r] for r in range(1, 8)] + [a_tail + b[0]], axis=0)
        theta = _sorted_top(cand, TOPK)[-1]
        top = a[0] + b[0]
        z = jnp.sum(jnp.where(cand >= theta, jnp.exp(cand - top), 0.0), axis=0, keepdims=True)
        w_ref[0, :, cs] = jnp.exp(s0 - a[0]) / z
        e1_ref[0, :, cs] = jnp.exp(s1 - b[0])
        th_ref[0, :, cs] = theta
        return carry

    lax.fori_loop(0, n_chunks, chunk, 0)


def _peer_route(x, norm_w, wqT, keys_bf16):
    t, d = x.shape
    tm = TOKEN_TILE
    nq = wqT.shape[0]
    per_head = jax.ShapeDtypeStruct((N_PH, N_KEYS, t), F32)
    head_spec = pl.BlockSpec((1, N_KEYS, tm), lambda i, h: (h, 0, i))
    return pl.pallas_call(
        _route_kernel,
        grid=(t // tm, N_PH),
        in_specs=[pl.BlockSpec((tm, d), lambda i, h: (i, 0)),
                  pl.BlockSpec((1, d), lambda i, h: (0, 0)),
                  pl.BlockSpec((nq, d), lambda i, h: (0, 0)),
                  pl.BlockSpec((1, 2, N_KEYS, D_QUERY // 2), lambda i, h: (h, 0, 0, 0))],
        out_specs=[pl.BlockSpec((d, tm), lambda i, h: (0, i)),
                   head_spec, head_spec, head_spec, head_spec,
                   pl.BlockSpec((1, 1, tm), lambda i, h: (h, 0, i))],
        out_shape=[jax.ShapeDtypeStruct((d, t), BF16), per_head, per_head, per_head, per_head,
                   jax.ShapeDtypeStruct((N_PH, 1, t), F32)],
        scratch_shapes=[pltpu.VMEM((nq, tm), F32)],
        compiler_params=pltpu.CompilerParams(dimension_semantics=("arbitrary", "arbitrary"),
                                             vmem_limit_bytes=VMEM_LIMIT),
        name="peer_route",
    )(x, norm_w.reshape(1, d), wqT, keys_bf16)


def _gelu_exact(x):
    return 0.5 * x * (1.0 + lax.erf(x * (1.0 / math.sqrt(2.0))))


def _gated_activations(c, rows, a_ref, w_ref, b_ref, e1_ref, th_ref, act_ref, p_ref, group):
    cs = pl.ds(pl.multiple_of(c * LANES, LANES), LANES)
    a8 = [a_ref[h, pl.ds(group, SUBLANES), cs] for h in range(N_PH)]
    w8 = [w_ref[h, pl.ds(group, SUBLANES), cs] for h in range(N_PH)]
    th = [th_ref[h, :, cs] for h in range(N_PH)]
    jb = 2 * SUBLANES
    for j0 in range(0, N_KEYS, jb):
        js = slice(j0, j0 + jb)
        b_blk = [b_ref[h, js, cs] for h in range(N_PH)]
        e_blk = [e1_ref[h, js, cs] for h in range(N_PH)]
        for n, r in enumerate(rows):
            gate = jnp.zeros((jb, LANES), F32)
            for h in range(N_PH):
                pair = b_blk[h] + a8[h][r:r + 1]
                gate = gate + jnp.where(pair >= th[h], e_blk[h] * w8[h][r:r + 1], 0.0)
            es = slice(n * N_KEYS + j0, n * N_KEYS + j0 + jb)
            p_ref[es, cs] = (gate * _gelu_exact(act_ref[es, cs])).astype(BF16)


def _half_step(rows, routing, group, h2T_ref, act_in, p_out, u_next_ref, act_next, vT_prev_ref, p_prev, acc_scr):
    def chunk(c, carry):
        _gated_activations(c, rows, *routing, act_in, p_out, group)
        return carry

    lax.fori_loop(0, act_in.shape[1] // LANES, chunk, 0)
    act_next[...] = jnp.dot(u_next_ref[...], h2T_ref[...], preferred_element_type=F32)
    acc_scr[...] += jnp.dot(vT_prev_ref[...], p_prev[...], preferred_element_type=F32)


def _dense_kernel(h2T_ref, a_ref, w_ref, b_ref, e1_ref, th_ref, u_first_ref, u_odd_ref, u_even_ref,
                  vT_prev_ref, vT_even_ref, vT_last_ref, x_ref, o_ref, acc_scr, act0, act1, p0, p1):
    k = pl.program_id(1)
    half = SUBLANES // 2
    group = pl.multiple_of(k * SUBLANES, SUBLANES)
    routing = (a_ref, w_ref, b_ref, e1_ref, th_ref)

    @pl.when(k == 0)
    def _():
        acc_scr[...] = jnp.zeros_like(acc_scr)
        p1[...] = jnp.zeros_like(p1)
        act0[...] = jnp.dot(u_first_ref[...], h2T_ref[...], preferred_element_type=F32)

    _half_step(range(0, half), routing, group, h2T_ref, act0, p0, u_odd_ref, act1, vT_prev_ref, p1, acc_scr)
    _half_step(range(half, SUBLANES), routing, group, h2T_ref, act1, p1, u_even_ref, act0, vT_even_ref, p0, acc_scr)

    @pl.when(k == pl.num_programs(1) - 1)
    def _():
        acc = acc_scr[...] + jnp.dot(vT_last_ref[...], p1[...], preferred_element_type=F32)
        o_ref[...] = x_ref[...] + acc.T


def _peer_dense(x, h2T, a, w, b, e1, th, u_bf16, vT_bf16):
    t, d = x.shape
    tm, te = TOKEN_TILE, EXPERT_TILE
    assert te == (SUBLANES // 2) * N_KEYS
    n_tiles = N_EXPERTS // te
    last = n_tiles - 1
    head_spec = pl.BlockSpec((N_PH, N_KEYS, tm), lambda i, k: (0, 0, i))
    u_spec = lambda f: pl.BlockSpec((te, d), lambda i, k: (f(k), 0))
    v_spec = lambda f: pl.BlockSpec((d, te), lambda i, k: (0, f(k)))
    return pl.pallas_call(
        _dense_kernel,
        grid=(t // tm, n_tiles // 2),
        in_specs=[pl.BlockSpec((d, tm), lambda i, k: (0, i)),
                  head_spec, head_spec, head_spec, head_spec,
                  pl.BlockSpec((N_PH, 1, tm), lambda i, k: (0, 0, i)),
                  u_spec(lambda k: 0), u_spec(lambda k: 2 * k + 1), u_spec(lambda k: jnp.minimum(2 * k + 2, last)),
                  v_spec(lambda k: jnp.maximum(2 * k - 1, 0)), v_spec(lambda k: 2 * k), v_spec(lambda k: last),
                  pl.BlockSpec((tm, d), lambda i, k: (i, 0))],
        out_specs=pl.BlockSpec((tm, d), lambda i, k: (i, 0)),
        out_shape=jax.ShapeDtypeStruct((t, d), F32),
        scratch_shapes=[pltpu.VMEM((d, tm), F32), pltpu.VMEM((te, tm), F32), pltpu.VMEM((te, tm), F32),
                        pltpu.VMEM((te, tm), BF16), pltpu.VMEM((te, tm), BF16)],
        compiler_params=pltpu.CompilerParams(dimension_semantics=("arbitrary", "arbitrary"),
                                             vmem_limit_bytes=VMEM_LIMIT),
        name="peer_dense",
    )(h2T, a, w, b, e1, th, u_bf16, u_bf16, u_bf16, vT_bf16, vT_bf16, vT_bf16, x)


def _peer_block(x, norm_w, wq, keys, u, v):
    h2T, a, b, e1, w, th = _peer_route(x, norm_w, wq.T.astype(BF16), keys.astype(BF16))
    return _peer_dense(x, h2T, a, w, b, e1, th, u.astype(BF16), v.T.astype(BF16))


ATT_TQ = 256
ATT_WIN = MAX_WINDOW + ATT_TQ
ATT_W = H_C * HD_C


def _log_multiplicity(dist):
    count = jnp.zeros(dist.shape, F32)
    for window, dil in DIL_GROUPS:
        count = count + ((dist >= 0) & (dist <= window) & (dist % dil == 0)).astype(F32)
    return jnp.where(count > 0, jnp.log(jnp.maximum(count, 1.0)), -jnp.inf)


def _head_norm_store(acc, nw_ref, o_ref):
    lane = lax.broadcasted_iota(jnp.int32, (1, ATT_W), 1)
    sq = acc * acc
    inv = jnp.zeros_like(acc)
    for h in range(H_C):
        hm = (lane >= h * HD_C) & (lane < (h + 1) * HD_C)
        ms = jnp.sum(jnp.where(hm, sq, 0.0), axis=-1, keepdims=True) * (1.0 / HD_C)
        inv = inv + jnp.where(hm, lax.rsqrt(ms + EPS), 0.0)
    o_ref[...] = acc * inv * nw_ref[...]


def _attn_prompt_kernel(q_ref, k_ref, v_ref, bias_ref, nw_ref, o_ref):
    i = pl.program_id(1)
    tq = q_ref.shape[0]
    lk = k_ref.shape[0]
    w0 = pl.multiple_of(jnp.clip(i * tq - MAX_WINDOW, 0, lk - ATT_WIN), tq)
    kw = k_ref[pl.ds(w0, ATT_WIN), :]
    vw = v_ref[pl.ds(w0, ATT_WIN), :]
    q = q_ref[...] * (HD_C ** -0.5)
    lane = lax.broadcasted_iota(jnp.int32, (1, ATT_W), 1)
    acc = jnp.zeros((tq, ATT_W), F32)
    for h in range(H_C):
        hm = (lane >= h * HD_C) & (lane < (h + 1) * HD_C)
        qh = jnp.where(hm, q, 0.0).astype(BF16)
        s = lax.dot_general(qh, kw, (((1,), (1,)), ((), ())), preferred_element_type=F32) + bias_ref[0]
        m = jnp.max(s, axis=-1, keepdims=True)
        p = jnp.exp(s - m)
        den = jnp.sum(p, axis=-1, keepdims=True)
        oh = jnp.dot(p.astype(BF16), vw, preferred_element_type=F32) / den
        acc = acc + jnp.where(hm, oh, 0.0)
    _head_norm_store(acc, nw_ref, o_ref)


def _attn_prompt(proj, kv_bf16, onorm_c_w, batch, seq):
    nblk = seq // ATT_TQ
    n_bias = MAX_WINDOW // ATT_TQ + 1
    delta = jnp.arange(n_bias, dtype=jnp.int32)[:, None, None] * ATT_TQ
    dist = (delta + jnp.arange(ATT_TQ, dtype=jnp.int32)[None, :, None]
            - jnp.arange(ATT_WIN, dtype=jnp.int32)[None, None, :])
    bias = _log_multiplicity(dist)
    qcol = P_CQKV // ATT_W
    return pl.pallas_call(
        _attn_prompt_kernel,
        grid=(batch, nblk),
        in_specs=[pl.BlockSpec((ATT_TQ, ATT_W), lambda b, i: (b * nblk + i, qcol)),
                  pl.BlockSpec((seq, ATT_W), lambda b, i: (b, 0)),
                  pl.BlockSpec((seq, ATT_W), lambda b, i: (b, 1)),
                  pl.BlockSpec((1, ATT_TQ, ATT_WIN), lambda b, i: (jnp.minimum(i, n_bias - 1), 0, 0)),
                  pl.BlockSpec((1, ATT_W), lambda b, i: (0, 0))],
        out_specs=pl.BlockSpec((ATT_TQ, ATT_W), lambda b, i: (b * nblk + i, 0)),
        out_shape=jax.ShapeDtypeStruct((batch * seq, ATT_W), F32),
        compiler_params=pltpu.CompilerParams(dimension_semantics=("arbitrary", "arbitrary"),
                                             vmem_limit_bytes=VMEM_LIMIT),
        name="attn_prompt",
    )(proj, kv_bf16, kv_bf16, bias, jnp.tile(onorm_c_w, H_C).reshape(1, ATT_W))


HGRN_W = H_B * DK_B
HGRN_VW = H_B * DV_B


def _hgrn_kernel(q_ref, f_ref, i_ref, gate_ref, lb_ref, nw_ref, s0_ref, o_ref, st_ref,
                 st_scr, qt_scr, kt_scr, egl_scr, o_scr, *, c):
    tl = q_ref.shape[0]
    nc = tl // c

    @pl.when(pl.program_id(1) == 0)
    def _():
        st_scr[...] = s0_ref[0]

    z = f_ref[...]
    lb = lb_ref[...]
    log_sig = jnp.minimum(z, 0.0) - jnp.log1p(jnp.exp(-jnp.abs(z)))
    la = jnp.log(lb)
    lc = jnp.log1p(-lb) + log_sig
    logf = jnp.maximum(la, lc) + jnp.log1p(jnp.exp(-jnp.abs(la - lc)))
    kk = (1.0 - lb) * jax.nn.sigmoid(-z)

    rowmod = lax.broadcasted_iota(jnp.int32, (tl, 1), 0) & (c - 1)
    g = logf
    sh = 1
    while sh < c:
        g = g + jnp.where(rowmod >= sh, pltpu.roll(g, sh, axis=0), 0.0)
        sh *= 2

    g3 = g.reshape(nc, c, HGRN_W)
    glast = g3[:, c - 1:c, :]
    q3 = (q_ref[...] * (DK_B ** -0.5)).reshape(nc, c, HGRN_W)
    k3 = kk.reshape(nc, c, HGRN_W)
    qt_scr[...] = (q3 * jnp.exp(g3)).reshape(tl, HGRN_W)
    kt_scr[...] = (k3 * jnp.exp(glast - g3)).reshape(tl, HGRN_W)
    egl_scr[...] = jnp.exp(glast)

    tpos = lax.broadcasted_iota(jnp.int32, (1, c, 1), 1)
    v3 = [i_ref[:, h * DV_B:(h + 1) * DV_B].reshape(nc, c, DV_B) for h in range(H_B)]
    o3 = [jnp.zeros((nc, c, DV_B), F32) for _ in range(H_B)]
    for s in range(c):
        dec = jnp.exp(jnp.where(tpos >= s, g3 - g3[:, s:s + 1, :], -jnp.inf))
        term = q3 * k3[:, s:s + 1, :] * dec
        for h in range(H_B):
            red = jnp.sum(term[:, :, h * DK_B:(h + 1) * DK_B], axis=-1, keepdims=True)
            o3[h] = o3[h] + red * v3[h][:, s:s + 1, :]
    for h in range(H_B):
        o_scr[h] = o3[h].reshape(tl, DV_B)

    def chunk(ci, carry):
        r0 = pl.multiple_of(ci * c, c)
        egl = egl_scr[ci]
        for h in range(H_B):
            ks = slice(h * DK_B, (h + 1) * DK_B)
            qc = qt_scr[pl.ds(r0, c), ks].astype(BF16)
            kc = kt_scr[pl.ds(r0, c), ks].astype(BF16)
            vc = i_ref[pl.ds(r0, c), h * DV_B:(h + 1) * DV_B].astype(BF16)
            st = st_scr[h]
            o_scr[h, pl.ds(r0, c), :] += lax.dot_general(qc, st.astype(BF16), (((1,), (1,)), ((), ())),
                                                         preferred_element_type=F32)
            st_scr[h] = st * egl[:, ks] + lax.dot_general(vc, kc, (((0,), (0,)), ((), ())),
                                                          preferred_element_type=F32)
        return carry

    lax.fori_loop(0, nc, chunk, 0)
    st_ref[0] = st_scr[...]

    for h in range(H_B):
        vs = slice(h * DV_B, (h + 1) * DV_B)
        o = o_scr[h]
        ms = jnp.mean(o * o, axis=-1, keepdims=True)
        o_ref[:, vs] = o * lax.rsqrt(ms + EPS) * nw_ref[...] * jax.nn.silu(gate_ref[:, vs])


def _hgrn(proj, row0, batch, seq, lb, onorm_b_w, s0_t, tl, c):
    nblk = seq // tl
    rb = row0 // tl
    col = lambda off, w: (lambda b, l: (rb + b * nblk + l, off // w))
    return pl.pallas_call(
        functools.partial(_hgrn_kernel, c=c),
        grid=(batch, nblk),
        in_specs=[pl.BlockSpec((tl, HGRN_W), col(P_BQ, HGRN_W)),
                  pl.BlockSpec((tl, HGRN_W), col(P_BF, HGRN_W)),
                  pl.BlockSpec((tl, HGRN_VW), col(P_BI, HGRN_VW)),
                  pl.BlockSpec((tl, HGRN_VW), col(P_BG, HGRN_VW)),
                  pl.BlockSpec((1, HGRN_W), lambda b, l: (0, 0)),
                  pl.BlockSpec((1, DV_B), lambda b, l: (0, 0)),
                  pl.BlockSpec((1, H_B, DV_B, DK_B), lambda b, l: (b, 0, 0, 0))],
        out_specs=[pl.BlockSpec((tl, HGRN_VW), lambda b, l: (b * nblk + l, 0)),
                   pl.BlockSpec((1, H_B, DV_B, DK_B), lambda b, l: (b, 0, 0, 0))],
        out_shape=[jax.ShapeDtypeStruct((batch * seq, HGRN_VW), F32),
                   jax.ShapeDtypeStruct((batch, H_B, DV_B, DK_B), F32)],
        scratch_shapes=[pltpu.VMEM((H_B, DV_B, DK_B), F32), pltpu.VMEM((tl, HGRN_W), F32),
                        pltpu.VMEM((tl, HGRN_W), F32), pltpu.VMEM((tl // c, 1, HGRN_W), F32),
                        pltpu.VMEM((H_B, tl, DV_B), F32)],
        compiler_params=pltpu.CompilerParams(dimension_semantics=("arbitrary", "arbitrary"),
                                             vmem_limit_bytes=VMEM_LIMIT),
        name="hgrn2",
    )(proj, proj, proj, proj, lb.reshape(1, HGRN_W), onorm_b_w.reshape(1, DV_B), s0_t)


def _rmsnorm(x, w):
    return x * lax.rsqrt(jnp.mean(x * x, axis=-1, keepdims=True) + EPS) * w


def _l2norm(x):
    return x * lax.rsqrt(jnp.sum(x * x, axis=-1, keepdims=True) + EPS)


def _pad_seq(t, pad):
    return jnp.pad(t, [(0, 0), (0, pad)] + [(0, 0)] * (t.ndim - 2))


def _to_chunks(t, c):
    b, lp = t.shape[:2]
    t = t.reshape((b, lp // c, c) + t.shape[2:])
    return jnp.moveaxis(t, (1, 3), (0, 2))


def _from_chunks(t):
    t = jnp.moveaxis(t, (0, 2), (1, 3))
    b, n, c = t.shape[:3]
    return t.reshape((b, n * c) + t.shape[3:])


def _causal_conv(x, buf, w):
    L = x.shape[1]
    xp = jnp.concatenate([buf, x], axis=1)
    y = xp[:, 0:L] * w[0]
    for j in range(1, CONV_W):
        y = y + xp[:, j:j + L] * w[j]
    return y, xp[:, -(CONV_W - 1):]


def _gated_delta_rule(q, k, v, beta, g, S0):
    B, L, H, dk = q.shape
    c = min(CHUNK_A, L)
    pad = (-L) % c
    q, k, v, beta, g = [_pad_seq(t, pad) for t in (q * dk ** -0.5, k, v, beta, g)]
    qc, kc, vc = _to_chunks(q, c), _to_chunks(k, c), _to_chunks(v, c)
    bc = _to_chunks(beta, c)
    gcum = jnp.cumsum(_to_chunks(g, c), axis=-1)
    strict = jnp.tril(jnp.ones((c, c), bool), -1)
    incl = jnp.tril(jnp.ones((c, c), bool))
    diff = gcum[..., :, None] - gcum[..., None, :]
    d_strict = jnp.exp(jnp.where(strict, diff, -jnp.inf))
    d_incl = jnp.exp(jnp.where(incl, diff, -jnp.inf))
    kb = kc * bc[..., None]
    a_mat = jnp.eye(c, dtype=F32) + jnp.einsum('nbhtk,nbhsk->nbhts', kb, kc) * d_strict
    u_c = lax.linalg.triangular_solve(a_mat, vc * bc[..., None], left_side=True, lower=True, unit_diagonal=True)
    w_c = lax.linalg.triangular_solve(a_mat, kb * jnp.exp(gcum)[..., None], left_side=True, lower=True,
                                      unit_diagonal=True)
    att = jnp.einsum('nbhtk,nbhsk->nbhts', qc, kc) * d_incl
    qg = qc * jnp.exp(gcum)[..., None]
    kg = kc * jnp.exp(gcum[..., -1:] - gcum)[..., None]
    glast = jnp.exp(gcum[..., -1])

    def step(S, xs):
        u_, w_, att_, qg_, kg_, gl_ = xs
        v_new = u_ - jnp.einsum('bhtk,bhkv->bhtv', w_, S)
        o = jnp.einsum('bhtk,bhkv->bhtv', qg_, S) + jnp.einsum('bhts,bhsv->bhtv', att_, v_new)
        S = S * gl_[..., None, None] + jnp.einsum('bhsk,bhsv->bhkv', kg_, v_new)
        return S, o

    S, o = lax.scan(step, S0, (u_c, w_c, att, qg, kg, glast))
    return _from_chunks(o)[:, :L], S


def _dilated_window_attention(q, k, v, q_start):
    B, Lq, H, hd = q.shape
    Lk = k.shape[1]
    blk = min(Q_BLOCK, Lq)
    pad = (-Lq) % blk
    nb = (Lq + pad) // blk
    qb = jnp.moveaxis(_pad_seq(q, pad).reshape(B, nb, blk, H, hd), 1, 0)
    starts = q_start + jnp.arange(nb, dtype=jnp.int32) * blk
    scale = hd ** -0.5

    def block(args):
        qblk, s0 = args
        qpos = s0 + jnp.arange(blk, dtype=jnp.int32)
        outs, lses = [], []
        for window, dil in DIL_GROUPS:
            idx = qpos[:, None] - dil * jnp.arange(window // dil + 1, dtype=jnp.int32)[None, :]
            valid = idx >= 0
            idx = jnp.clip(idx, 0, Lk - 1)
            kg = k[:, idx]
            vg = v[:, idx]
            s = jnp.einsum('bqhd,bqmhd->bqhm', qblk, kg).astype(F32) * scale
            s = jnp.where(valid[None, :, None, :], s, -jnp.inf)
            mx = jnp.max(s, axis=-1, keepdims=True)
            e = jnp.exp(s - mx)
            den = jnp.sum(e, axis=-1, keepdims=True)
            outs.append(jnp.einsum('bqhm,bqmhd->bqhd', e / den, vg.astype(F32)))
            lses.append((mx + jnp.log(den))[..., 0])
        wts = jax.nn.softmax(jnp.stack(lses, axis=-1), axis=-1)
        return jnp.einsum('bqhg,gbqhd->bqhd', wts, jnp.stack(outs, axis=0))

    out = lax.map(block, (qb, starts))
    return jnp.moveaxis(out, 0, 1).reshape(B, nb * blk, H, hd)[:, :Lq]


def _group_a(proj, conv_buf, s_delta, conv_a_w, a_log, dt_bias, onorm_a_w):
    B, L, _ = proj.shape
    a_qkv = proj[..., P_QKV_A:P_QKV_A + CONV_CH]
    a_z = proj[..., P_Z:P_Z + H_A * DV_A]
    a_b = proj[..., P_BA:P_BA + H_A]
    a_a = proj[..., P_BA + H_A:P_BA + 2 * H_A]
    a_qkv, new_conv = _causal_conv(a_qkv, conv_buf, conv_a_w)
    a_qkv = jax.nn.silu(a_qkv)
    aq, ak, av = jnp.split(a_qkv, [H_A * DK_A, 2 * H_A * DK_A], axis=-1)
    aq = _l2norm(aq.reshape(B, L, H_A, DK_A))
    ak = _l2norm(ak.reshape(B, L, H_A, DK_A))
    av = av.reshape(B, L, H_A, DV_A)
    beta = jax.nn.sigmoid(a_b)
    g = -jnp.exp(a_log) * jax.nn.softplus(a_a + dt_bias)
    o_a, new_delta = _gated_delta_rule(aq, ak, av, beta, g, s_delta)
    o_a = _rmsnorm(o_a, onorm_a_w) * jax.nn.silu(a_z.reshape(B, L, H_A, DV_A))
    return o_a.reshape(B * L, H_A * DV_A), new_conv, new_delta


def _group_c_sample(proj, k_buf, v_buf, onorm_c_w):
    B, L, _ = proj.shape
    cqkv = proj[..., P_CQKV:P_CQKV + 3 * ATT_W].reshape(B, L, 3, H_C, HD_C)
    cq, ck, cv = cqkv[:, :, 0], cqkv[:, :, 1], cqkv[:, :, 2]
    k_all = jnp.concatenate([k_buf, ck], axis=1)
    v_all = jnp.concatenate([v_buf, cv], axis=1)
    o_c = _rmsnorm(_dilated_window_attention(cq, k_all, v_all, k_buf.shape[1]), onorm_c_w)
    keep = min(MAX_WINDOW, k_all.shape[1])
    return o_c.reshape(B * L, ATT_W), k_all[:, -keep:], v_all[:, -keep:]


def kernel(x_prompt, x_sample, state_conv_a, state_delta, state_hgrn, cache_swa_k, cache_swa_v, norm1_w, w_in,
           conv_a_w, a_log, dt_bias, onorm_a_w, lb_logits, onorm_b_w, onorm_c_w, w_out, norm2_w, peer_wq, peer_keys,
           peer_u, peer_v, final_norm_w):
    depth = w_in.shape[0]
    bp, lp, d = x_prompt.shape
    bs, ls, _ = x_sample.shape
    tp, ts = bp * lp, bs * ls
    t_pad = -(-(tp + ts) // TOKEN_TILE) * TOKEN_TILE
    sm = jax.nn.softmax(lb_logits, axis=0)
    lower_bounds = jnp.clip(jnp.cumsum(sm, axis=0) - sm[:1], 0.0, 1.0 - 1e-6)
    perm = _in_proj_perm()
    col_src = jnp.asarray(np.maximum(perm, 0))
    col_ok = jnp.asarray(perm >= 0)
    keep_p = min(MAX_WINDOW, lp)

    x = jnp.concatenate([x_prompt.reshape(tp, d), x_sample.reshape(ts, d), jnp.zeros((t_pad - tp - ts, d), F32)],
                        axis=0)
    outs = [[] for _ in range(10)]
    for li in range(depth):
        w_in_p = jnp.where(col_ok[None, :], w_in[li][:, col_src], 0.0).astype(BF16)
        proj = _norm_matmul(x, norm1_w[li], w_in_p, tn=640)
        proj_p = proj[:tp].reshape(bp, lp, -1)
        proj_s = proj[tp:tp + ts].reshape(bs, ls, -1)
        a_w = (conv_a_w[li], a_log[li], dt_bias[li], onorm_a_w[li])

        oa_p, c1, d1 = _group_a(proj_p, jnp.zeros((bp, CONV_W - 1, CONV_CH), F32),
                                jnp.zeros((bp, H_A, DK_A, DV_A), F32), *a_w)
        oa_s, c2, d2 = _group_a(proj_s, state_conv_a[li], state_delta[li], *a_w)

        ob_p, h1 = _hgrn(proj, 0, bp, lp, lower_bounds[li], onorm_b_w[li], jnp.zeros((bp, H_B, DV_B, DK_B), F32),
                         256, CHUNK_B)
        ob_s, h2 = _hgrn(proj, tp, bs, ls, lower_bounds[li], onorm_b_w[li], jnp.swapaxes(state_hgrn[li], 2, 3),
                         ls, min(CHUNK_B, ls))

        kv_p = proj[:tp, P_CQKV + ATT_W:P_CQKV + 3 * ATT_W]
        oc_p = _attn_prompt(proj, kv_p.astype(BF16), onorm_c_w[li], bp, lp)
        kv_keep = kv_p.reshape(bp, lp, 2, H_C, HD_C)[:, lp - keep_p:]
        k1, v1 = kv_keep[:, :, 0], kv_keep[:, :, 1]
        oc_s, k2, v2 = _group_c_sample(proj_s, cache_swa_k[li], cache_swa_v[li], onorm_c_w[li])

        mix = jnp.concatenate([jnp.concatenate([oa_p, ob_p, oc_p], axis=1),
                               jnp.concatenate([oa_s, ob_s, oc_s], axis=1),
                               jnp.zeros((t_pad - tp - ts, d), F32)], axis=0)
        x = _out_proj(x, mix, w_out[li].astype(BF16))
        x = _peer_block(x, norm2_w[li], peer_wq[li], peer_keys[li], peer_u[li], peer_v[li])
        for lst, val in zip(outs, (c1, c2, d1, d2, jnp.swapaxes(h1, 2, 3), jnp.swapaxes(h2, 2, 3), k1, v1, k2, v2)):
            lst.append(val)
    y = _final_norm(x, final_norm_w)
    return (y[:tp].reshape(bp, lp, d), y[tp:tp + ts].reshape(bs, ls, d)) + tuple(jnp.stack(o) for o in outs)
```
